```python
import math
import jax
import jax.numpy as jnp
from jax import lax
import numpy as np

D_MODEL = 4096
BATCH = 4
SEQ = 2048
DEPTH = 2
DEC_BATCH = 8
DEC_SEQ = 1
PAST_LEN = 16384
PAGE_SIZE = 128

HEAD_DIM = 128
MIX_WIDTH = D_MODEL
MOBA_WIDTH = MIX_WIDTH // 2
MOBA_HEADS = MOBA_WIDTH // HEAD_DIM
DIFF_WIDTH = MIX_WIDTH - MOBA_WIDTH
DIFF_V_DIM = 2 * HEAD_DIM
DIFF_HEADS = DIFF_WIDTH // DIFF_V_DIM
DIFF_QK_WIDTH = 2 * DIFF_HEADS * HEAD_DIM
IN_PROJ_WIDTH = 3 * MOBA_WIDTH + 2 * DIFF_QK_WIDTH + DIFF_WIDTH
MOBA_BLOCK = 256
MOBA_TOPK = 3
MOBA_Q_CHUNK = 16
PAGES_PER_BLOCK = MOBA_BLOCK // PAGE_SIZE
ATTN_Q_CHUNK = 128
ROPE_THETA = 10000.0
N_GROUPS = 4
EXPERTS_PER_GROUP = 8
N_EXPERTS = N_GROUPS * EXPERTS_PER_GROUP
EXPERT_TOPK = 2
EXPERT_FF = D_MODEL // 4
DEEPNORM_ALPHA = (2 * DEPTH) ** 0.25
DEEPNORM_BETA = (8 * DEPTH) ** -0.25
LN_EPS = 1e-5
SUBLN_EPS = 1e-5
F32 = jnp.float32

kernel_name = 'hybrid_moba_diffattn_hmoe_step'


def layer_norm(x, g, b):
    x32 = x.astype(F32)
    mu = jnp.mean(x32, axis=-1, keepdims=True)
    var = jnp.mean(jnp.square(x32 - mu), axis=-1, keepdims=True)
    return ((x32 - mu) * lax.rsqrt(var + LN_EPS) * g.astype(F32) + b.astype(F32)).astype(x.dtype)


def rope(x, pos):
    half = HEAD_DIM // 2
    inv_freq = ROPE_THETA ** (-jnp.arange(half, dtype=F32) / half)
    ang = pos.astype(F32)[:, None] * inv_freq[None, :]
    cos = jnp.cos(ang)[None, :, None, :]
    sin = jnp.sin(ang)[None, :, None, :]
    x32 = x.astype(F32)
    x1, x2 = x32[..., :half], x32[..., half:]
    return jnp.concatenate([x1 * cos - x2 * sin, x2 * cos + x1 * sin], axis=-1).astype(x.dtype)


def project(x, pos, w_in):
    B, T, _ = x.shape
    z = jnp.einsum('btd,de->bte', x, w_in)
    cuts = [MOBA_WIDTH, 2 * MOBA_WIDTH, 3 * MOBA_WIDTH,
            3 * MOBA_WIDTH + DIFF_QK_WIDTH, 3 * MOBA_WIDTH + 2 * DIFF_QK_WIDTH]
    qm, km, vm, qd, kd, vd = jnp.split(z, cuts, axis=-1)
    qm = rope(qm.reshape(B, T, MOBA_HEADS, HEAD_DIM), pos)
    km = rope(km.reshape(B, T, MOBA_HEADS, HEAD_DIM), pos)
    vm = vm.reshape(B, T, MOBA_HEADS, HEAD_DIM)
    qd = rope(qd.reshape(B, T, 2 * DIFF_HEADS, HEAD_DIM), pos).reshape(B, T, DIFF_HEADS, 2, HEAD_DIM)
    kd = rope(kd.reshape(B, T, 2 * DIFF_HEADS, HEAD_DIM), pos).reshape(B, T, DIFF_HEADS, 2, HEAD_DIM)
    vd = vd.reshape(B, T, DIFF_HEADS, DIFF_V_DIM)
    return qm, km, vm, qd, kd, vd


def merge_heads(om, od, w_out):
    B, T = om.shape[:2]
    cat = jnp.concatenate([om.reshape(B, T, -1), od.reshape(B, T, -1).astype(om.dtype)], axis=-1)
    return jnp.einsum('bte,ed->btd', cat, w_out)


def to_blocks(x):
    B, L = x.shape[:2]
    nb = -(-L // MOBA_BLOCK)
    x = jnp.pad(x, [(0, 0), (0, nb * MOBA_BLOCK - L)] + [(0, 0)] * (x.ndim - 2))
    return x.reshape((B, nb, MOBA_BLOCK) + x.shape[2:])


def gather_blocks(xr, idx):
    b = jnp.arange(idx.shape[0])[:, None, None, None]
    h = jnp.arange(idx.shape[2])[None, None, :, None]
    return xr[b, idx, :, h, :]


def moba_attend(q, pos, k_mean, k_rows, v_rows):
    B, T, H, dh = q.shape
    nb = k_mean.shape[1]
    own = pos // MOBA_BLOCK
    gate = jnp.einsum('bthd,bnhd->bthn', q.astype(F32), k_mean)
    fully_past = jnp.arange(nb)[None, :] < own[:, None]
    gate = jnp.where(fully_past[None, :, None, :], gate, -jnp.inf)
    top_val, top_idx = lax.top_k(gate, min(MOBA_TOPK, nb))
    own_b = jnp.broadcast_to(own[None, :, None, None], (B, T, H, 1)).astype(top_idx.dtype)
    idx = jnp.concatenate([top_idx, own_b], axis=-1)
    keep = jnp.concatenate([jnp.isfinite(top_val), jnp.ones((B, T, H, 1), dtype=bool)], axis=-1)
    kg = k_rows(idx)
    vg = v_rows(idx)
    kpos = idx[..., None] * MOBA_BLOCK + jnp.arange(MOBA_BLOCK)
    mask = keep[..., None] & (kpos <= pos[None, :, None, None, None])
    s = jnp.einsum('bthd,bthjkd->bthjk', q, kg.astype(q.dtype)).astype(F32) * (HEAD_DIM ** -0.5)
    s = jnp.where(mask, s, -jnp.inf).reshape(B, T, H, -1)
    p = jax.nn.softmax(s, axis=-1).astype(vg.dtype)
    out = jnp.einsum('bthn,bthnd->bthd', p, vg.reshape(B, T, H, -1, dh))
    return out.astype(q.dtype)


def moba_prompt(q, k, v):
    B, S, H, dh = q.shape
    kr = to_blocks(k)
    vr = to_blocks(v)
    k_mean = jnp.mean(kr.astype(F32), axis=2)
    n = S // MOBA_Q_CHUNK
    qc = jnp.swapaxes(q.reshape(B, n, MOBA_Q_CHUNK, H, dh), 0, 1)

    def step(args):
        c, q_c = args
        pos = c * MOBA_Q_CHUNK + jnp.arange(MOBA_Q_CHUNK, dtype=jnp.int32)
        return moba_attend(q_c, pos, k_mean,
                           lambda idx: gather_blocks(kr, idx),
                           lambda idx: gather_blocks(vr, idx))

    out = lax.map(step, (jnp.arange(n, dtype=jnp.int32), qc))
    return jnp.swapaxes(out, 0, 1).reshape(B, S, H, dh)


def moba_sample(q, k_new, v_new, cache_k, cache_v, layer, page_table):
    DB, T, H, dh = q.shape
    n_pages = page_table.shape[1]
    past = n_pages * PAGE_SIZE
    k_past = cache_k[layer, page_table].reshape(DB, past, H, dh)
    kr = to_blocks(jnp.concatenate([k_past, k_new.astype(k_past.dtype)], axis=1))
    k_mean = jnp.mean(kr.astype(F32), axis=2)
    n_new = -(-T // PAGE_SIZE)
    v_new_pages = jnp.pad(v_new, ((0, 0), (0, n_new * PAGE_SIZE - T), (0, 0), (0, 0)))
    v_new_pages = v_new_pages.reshape(DB, n_new, PAGE_SIZE, H, dh).astype(cache_v.dtype)
    b = jnp.arange(DB)[:, None, None, None, None]
    h = jnp.arange(H)[None, None, :, None, None, None]
    r = jnp.arange(PAGE_SIZE)

    def v_rows(idx):
        page = idx[..., None] * PAGES_PER_BLOCK + jnp.arange(PAGES_PER_BLOCK)
        phys = page_table[b, jnp.clip(page, 0, n_pages - 1)]
        from_cache = cache_v[layer, phys[..., None], r, h]
        new_p = jnp.clip(page - n_pages, 0, n_new - 1)
        from_new = v_new_pages[b[..., None], new_p[..., None], r, h]
        rows = jnp.where((page < n_pages)[..., None, None], from_cache, from_new)
        return rows.reshape(idx.shape + (MOBA_BLOCK, dh))

    pos = past + jnp.arange(T, dtype=jnp.int32)
    return moba_attend(q, pos, k_mean, lambda idx: gather_blocks(kr, idx), v_rows)


def diff_core(q, k, v, qpos, kpos, lam, lam_init, g_sub):
    s = jnp.einsum('bqhcd,bkhcd->bhcqk', q, k.astype(q.dtype)).astype(F32) * (HEAD_DIM ** -0.5)
    causal = kpos[None, :] <= qpos[:, None]
    s = jnp.where(causal, s, -jnp.inf)
    p = jax.nn.softmax(s, axis=-1)
    a = p[:, :, 0] - lam * p[:, :, 1]
    o = jnp.einsum('bhqk,bkhe->bqhe', a.astype(v.dtype), v).astype(F32)
    o = o * lax.rsqrt(jnp.mean(jnp.square(o), axis=-1, keepdims=True) + SUBLN_EPS)
    return (o * g_sub.astype(F32) * (1.0 - lam_init)).astype(q.dtype)


def diff_prompt(q, k, v, lam, lam_init, g_sub):
    B, S = q.shape[:2]
    n = S // ATTN_Q_CHUNK
    qc = jnp.swapaxes(q.reshape((B, n, ATTN_Q_CHUNK) + q.shape[2:]), 0, 1)
    kpos = jnp.arange(S, dtype=jnp.int32)

    def step(args):
        c, q_c = args
        qpos = c * ATTN_Q_CHUNK + jnp.arange(ATTN_Q_CHUNK, dtype=jnp.int32)
        return diff_core(q_c, k, v, qpos, kpos, lam, lam_init, g_sub)

    out = lax.map(step, (jnp.arange(n, dtype=jnp.int32), qc))
    return jnp.swapaxes(out, 0, 1).reshape(B, S, DIFF_HEADS, DIFF_V_DIM)


def diff_sample(q, k_new, v_new, cache_k, cache_v, layer, page_table, lam, lam_init, g_sub):
    DB, T = q.shape[:2]
    past = page_table.shape[1] * PAGE_SIZE
    k_past = cache_k[layer, page_table].reshape(DB, past, DIFF_HEADS, 2, HEAD_DIM)
    v_past = cache_v[layer, page_table].reshape(DB, past, DIFF_HEADS, DIFF_V_DIM)
    k = jnp.concatenate([k_past, k_new.astype(k_past.dtype)], axis=1)
    v = jnp.concatenate([v_past, v_new.astype(v_past.dtype)], axis=1)
    qpos = past + jnp.arange(T, dtype=jnp.int32)
    kpos = jnp.arange(past + T, dtype=jnp.int32)
    return diff_core(q, k, v, qpos, kpos, lam, lam_init, g_sub)


def hier_moe(h, w_rg, b_rg, w_re, b_re, w_g, w_u, w_d):
    shp = h.shape
    t = h.reshape(-1, shp[-1])
    n = t.shape[0]
    pg = jax.nn.softmax(jnp.dot(t, w_rg).astype(F32) + b_rg.astype(F32), axis=-1)
    g_val, g_idx = lax.top_k(pg, 1)
    le = (jnp.dot(t, w_re).astype(F32) + b_re.astype(F32)).reshape(n, N_GROUPS, EXPERTS_PER_GROUP)
    le = jnp.take_along_axis(le, g_idx[:, :, None], axis=1)[:, 0]
    pe = jax.nn.softmax(le, axis=-1)
    e_val, e_idx = lax.top_k(pe, EXPERT_TOPK)
    w = g_val * e_val / jnp.sum(e_val, axis=-1, keepdims=True)
    expert = g_idx * EXPERTS_PER_GROUP + e_idx
    gates = jnp.sum(jax.nn.one_hot(expert, N_EXPERTS, dtype=F32) * w[..., None], axis=1)
    a = jnp.einsum('nd,edf->nef', t, w_g)
    u = jnp.einsum('nd,edf->nef', t, w_u)
    hid = jax.nn.silu(a) * u * gates[:, :, None].astype(t.dtype)
    return jnp.einsum('nef,efd->nd', hid, w_d).reshape(shp)


def setup_inputs(seed: int = 0) -> dict:
    key = jax.random.key(seed)
    ks = jax.random.split(key, 32)
    n_pages = PAST_LEN // PAGE_SIZE
    n_used = DEC_BATCH * n_pages
    n_phys = n_used + max(1, n_used // 4)

    def nrm(k, shape, scale):
        return jax.random.normal(k, shape, F32) * scale

    d_in = D_MODEL ** -0.5
    return {
        'x_prompt': nrm(ks[0], (BATCH, SEQ, D_MODEL), 1.0),
        'x_sample': nrm(ks[1], (DEC_BATCH, DEC_SEQ, D_MODEL), 1.0),
        'cache_moba_k': nrm(ks[2], (DEPTH, n_phys, PAGE_SIZE, MOBA_HEADS, HEAD_DIM), 1.0),
        'cache_moba_v': nrm(ks[3], (DEPTH, n_phys, PAGE_SIZE, MOBA_HEADS, HEAD_DIM), 1.0),
        'cache_diff_k': nrm(ks[4], (DEPTH, n_phys, PAGE_SIZE, DIFF_HEADS, 2, HEAD_DIM), 1.0),
        'cache_diff_v': nrm(ks[5], (DEPTH, n_phys, PAGE_SIZE, DIFF_HEADS, DIFF_V_DIM), 1.0),
        'page_table': jax.random.permutation(ks[6], n_phys)[:n_used].reshape(DEC_BATCH, n_pages).astype(jnp.int32),
        'w_in': nrm(ks[7], (DEPTH, D_MODEL, IN_PROJ_WIDTH), d_in),
        'w_out': nrm(ks[8], (DEPTH, MIX_WIDTH, D_MODEL), MIX_WIDTH ** -0.5 * DEEPNORM_BETA),
        'lambda_q1': nrm(ks[9], (DEPTH, HEAD_DIM), 0.1),
        'lambda_k1': nrm(ks[10], (DEPTH, HEAD_DIM), 0.1),
        'lambda_q2': nrm(ks[11], (DEPTH, HEAD_DIM), 0.1),
        'lambda_k2': nrm(ks[12], (DEPTH, HEAD_DIM), 0.1),
        'subln_g': 1.0 + nrm(ks[13], (DEPTH, DIFF_V_DIM), 0.02),
        'ln1_g': 1.0 + nrm(ks[14], (DEPTH, D_MODEL), 0.02),
        'ln1_b': nrm(ks[15], (DEPTH, D_MODEL), 0.02),
        'w_router_group': nrm(ks[16], (DEPTH, D_MODEL, N_GROUPS), d_in),
        'b_router_group': nrm(ks[17], (DEPTH, N_GROUPS), 0.01),
        'w_router_expert': nrm(ks[18], (DEPTH, D_MODEL, N_EXPERTS), d_in),
        'b_router_expert': nrm(ks[19], (DEPTH, N_EXPERTS), 0.01),
        'w_exp_gate': nrm(ks[20], (DEPTH, N_EXPERTS, D_MODEL, EXPERT_FF), d_in),
        'w_exp_up': nrm(ks[21], (DEPTH, N_EXPERTS, D_MODEL, EXPERT_FF), d_in),
        'w_exp_down': nrm(ks[22], (DEPTH, N_EXPERTS, EXPERT_FF, D_MODEL), EXPERT_FF ** -0.5 * DEEPNORM_BETA),
        'ln2_g': 1.0 + nrm(ks[23], (DEPTH, D_MODEL), 0.02),
        'ln2_b': nrm(ks[24], (DEPTH, D_MODEL), 0.02),
    }


def reference(x_prompt, x_sample, cache_moba_k, cache_moba_v, cache_diff_k, cache_diff_v, page_table,
              w_in, w_out, lambda_q1, lambda_k1, lambda_q2, lambda_k2, subln_g, ln1_g, ln1_b,
              w_router_group, b_router_group, w_router_expert, b_router_expert,
              w_exp_gate, w_exp_up, w_exp_down, ln2_g, ln2_b):
    pos_p = jnp.arange(x_prompt.shape[1], dtype=jnp.int32)
    pos_s = PAST_LEN + jnp.arange(x_sample.shape[1], dtype=jnp.int32)
    hp, hs = x_prompt, x_sample
    mk_p, mv_p, dk_p, dv_p = [], [], [], []
    mk_s, mv_s, dk_s, dv_s = [], [], [], []
    for l in range(DEPTH):
        lam_init = 0.8 - 0.6 * math.exp(-0.3 * l)
        lam = (jnp.exp(jnp.sum(lambda_q1[l].astype(F32) * lambda_k1[l].astype(F32)))
               - jnp.exp(jnp.sum(lambda_q2[l].astype(F32) * lambda_k2[l].astype(F32))) + lam_init)

        qm, km, vm, qd, kd, vd = project(hp, pos_p, w_in[l])
        a = merge_heads(moba_prompt(qm, km, vm),
                        diff_prompt(qd, kd, vd, lam, lam_init, subln_g[l]), w_out[l])
        hp = layer_norm(DEEPNORM_ALPHA * hp + a, ln1_g[l], ln1_b[l])
        f = hier_moe(hp, w_router_group[l], b_router_group[l], w_router_expert[l], b_router_expert[l],
                     w_exp_gate[l], w_exp_up[l], w_exp_down[l])
        hp = layer_norm(DEEPNORM_ALPHA * hp + f, ln2_g[l], ln2_b[l])
        mk_p.append(km)
        mv_p.append(vm)
        dk_p.append(kd)
        dv_p.append(vd)

        qm, km, vm, qd, kd, vd = project(hs, pos_s, w_in[l])
        a = merge_heads(moba_sample(qm, km, vm, cache_moba_k, cache_moba_v, l, page_table),
                        diff_sample(qd, kd, vd, cache_diff_k, cache_diff_v, l, page_table,
                                    lam, lam_init, subln_g[l]), w_out[l])
        hs = layer_norm(DEEPNORM_ALPHA * hs + a, ln1_g[l], ln1_b[l])
        f = hier_moe(hs, w_router_group[l], b_router_group[l], w_router_expert[l], b_router_expert[l],
                     w_exp_gate[l], w_exp_up[l], w_exp_down[l])
        hs = layer_norm(DEEPNORM_ALPHA * hs + f, ln2_g[l], ln2_b[l])
        mk_s.append(km)
        mv_s.append(vm)
        dk_s.append(kd)
        dv_s.append(vd)

    return (hp, hs,
            jnp.stack(mk_p), jnp.stack(mv_p), jnp.stack(dk_p), jnp.stack(dv_p),
            jnp.stack(mk_s), jnp.stack(mv_s), jnp.stack(dk_s), jnp.stack(dv_s))
```

```python
import functools
import math

import jax
import jax.numpy as jnp
from jax import lax
from jax.experimental import pallas as pl
from jax.experimental.pallas import tpu as pltpu

F32 = jnp.float32
BF16 = jnp.bfloat16
HIGHEST = lax.Precision.HIGHEST

D_MODEL = 4096
HEAD_DIM = 128
MOBA_HEADS = 16
DIFF_HEADS = 8
DIFF_V_DIM = 256
GROUP_WIDTH = 2048
MOBA_BLOCK = 256
MOBA_TOPK = 3
PAGE_SIZE = 128
ROPE_THETA = 10000.0
N_GROUPS = 4
EXPERTS_PER_GROUP = 8
N_EXPERTS = 32
EXPERT_FF = 1024
DEPTH = 2
DEEPNORM_ALPHA = (2 * DEPTH) ** 0.25
LN_EPS = 1e-5
SUBLN_EPS = 1e-5
ATTN_SCALE = HEAD_DIM ** -0.5

ROW_TILE = 256
VMEM_LIMIT = 56 * 1024 * 1024


def _cparams(n_axes):
    return pltpu.CompilerParams(dimension_semantics=("arbitrary",) * n_axes,
                                vmem_limit_bytes=VMEM_LIMIT)


def _proj_kernel(x_ref, w_ref, cos_ref, sin_ref, o_ref, wb_ref, *, rope):
    @pl.when(pl.program_id(1) == 0)
    def _():
        wb_ref[...] = w_ref[...].astype(BF16)

    acc = jnp.dot(x_ref[...], wb_ref[...], preferred_element_type=F32)
    if rope:
        cos = cos_ref[...]
        sin = sin_ref[...]
        for h in range(acc.shape[1] // HEAD_DIM):
            z = acc[:, h * HEAD_DIM:(h + 1) * HEAD_DIM]
            o_ref[:, h * HEAD_DIM:(h + 1) * HEAD_DIM] = (
                z * cos + pltpu.roll(z, HEAD_DIM // 2, 1) * sin)
    else:
        o_ref[...] = acc


def _project(x_bf16, w, layer, col0, n_cols, cos, sin, *, rope, tm, tn):
    m, k = x_bf16.shape
    jb0 = col0 // tn
    return pl.pallas_call(
        functools.partial(_proj_kernel, rope=rope),
        grid=(n_cols // tn, m // tm),
        in_specs=[
            pl.BlockSpec((tm, k), lambda j, i: (i, 0)),
            pl.BlockSpec((None, k, tn), lambda j, i: (layer, 0, jb0 + j)),
            pl.BlockSpec((tm, HEAD_DIM), lambda j, i: (i, 0)),
            pl.BlockSpec((tm, HEAD_DIM), lambda j, i: (i, 0)),
        ],
        out_specs=pl.BlockSpec((tm, tn), lambda j, i: (i, j)),
        out_shape=jax.ShapeDtypeStruct((m, n_cols), F32),
        scratch_shapes=[pltpu.VMEM((k, tn), BF16)],
        compiler_params=_cparams(2),
        name="proj_rope" if rope else "proj",
    )(x_bf16, w, cos, sin)


def _flash_update(s, v_bf16, m_ref, l_ref, acc_ref):
    m_old = m_ref[...]
    m_new = jnp.maximum(m_old, jnp.max(s, axis=1, keepdims=True))
    alpha = jnp.exp(m_old - m_new)
    p = jnp.exp(s - m_new)
    l_ref[...] = alpha * l_ref[...] + jnp.sum(p, axis=1, keepdims=True)
    acc_ref[...] = alpha * acc_ref[...] + jnp.dot(p.astype(BF16), v_bf16, preferred_element_type=F32)
    m_ref[...] = m_new


def _nt_dot(a, b, **kw):
    return lax.dot_general(a, b, (((1,), (1,)), ((), ())), preferred_element_type=F32, **kw)


def _moba_prompt_kernel(q_ref, k_ref, v_ref, o_ref, m_ref, l_ref, acc_ref, *, n_blocks):
    qi = pl.program_id(2)
    q = q_ref[...]
    qb = q.astype(BF16)

    k_mean = jnp.mean(k_ref[...].reshape(n_blocks, MOBA_BLOCK, HEAD_DIM), axis=1)
    gate = _nt_dot(q, k_mean, precision=HIGHEST)
    blk = lax.broadcasted_iota(jnp.int32, gate.shape, 1)
    gate = jnp.where(blk < qi, gate, -jnp.inf)
    rank = jnp.zeros(gate.shape, jnp.int32)
    for n2 in range(n_blocks):
        col = gate[:, n2:n2 + 1]
        ahead = jnp.where(col > gate, 1, jnp.where(col == gate, jnp.where(n2 < blk, 1, 0), 0))
        rank = rank + ahead
    sel = jnp.where(rank < MOBA_TOPK, jnp.where(jnp.abs(gate) < jnp.inf, 1.0, 0.0), 0.0)

    row0 = pl.multiple_of(qi * MOBA_BLOCK, MOBA_BLOCK)
    kb = k_ref[pl.ds(row0, MOBA_BLOCK), :].astype(BF16)
    vb = v_ref[pl.ds(row0, MOBA_BLOCK), :].astype(BF16)
    s = _nt_dot(qb, kb) * ATTN_SCALE
    r_id = lax.broadcasted_iota(jnp.int32, s.shape, 0)
    c_id = lax.broadcasted_iota(jnp.int32, s.shape, 1)
    s = jnp.where(c_id <= r_id, s, -jnp.inf)
    m0 = jnp.max(s, axis=1, keepdims=True)
    p = jnp.exp(s - m0)
    m_ref[...] = m0
    l_ref[...] = jnp.sum(p, axis=1, keepdims=True)
    acc_ref[...] = jnp.dot(p.astype(BF16), vb, preferred_element_type=F32)

    for j in range(n_blocks - 1):
        @pl.when(j < qi)
        def _(j=j):
            kj = k_ref[j * MOBA_BLOCK:(j + 1) * MOBA_BLOCK, :].astype(BF16)
            vj = v_ref[j * MOBA_BLOCK:(j + 1) * MOBA_BLOCK, :].astype(BF16)
            sj = _nt_dot(qb, kj) * ATTN_SCALE
            sj = jnp.where(sel[:, j:j + 1] > 0.0, sj, -jnp.inf)
            _flash_update(sj, vj, m_ref, l_ref, acc_ref)

    o_ref[...] = (acc_ref[...] / l_ref[...]).astype(o_ref.dtype)


def _moba_prompt(q, k, v, batch, seq):
    nq = seq // MOBA_BLOCK
    return pl.pallas_call(
        functools.partial(_moba_prompt_kernel, n_blocks=nq),
        grid=(batch, MOBA_HEADS, nq),
        in_specs=[
            pl.BlockSpec((MOBA_BLOCK, HEAD_DIM), lambda b, h, i: (b * nq + i, h)),
            pl.BlockSpec((seq, HEAD_DIM), lambda b, h, i: (b, h)),
            pl.BlockSpec((seq, HEAD_DIM), lambda b, h, i: (b, h)),
        ],
        out_specs=pl.BlockSpec((MOBA_BLOCK, HEAD_DIM), lambda b, h, i: (b * nq + i, h)),
        out_shape=jax.ShapeDtypeStruct((batch * seq, MOBA_HEADS * HEAD_DIM), BF16),
        scratch_shapes=[pltpu.VMEM((MOBA_BLOCK, 1), F32), pltpu.VMEM((MOBA_BLOCK, 1), F32),
                        pltpu.VMEM((MOBA_BLOCK, HEAD_DIM), F32)],
        compiler_params=_cparams(3),
        name="moba_prompt",
    )(q, k, v)


def _lambda_value(lq1_ref, lk1_ref, lq2_ref, lk2_ref, lam_init):
    a = jnp.sum(lq1_ref[...] * lk1_ref[...], axis=1, keepdims=True)
    b = jnp.sum(lq2_ref[...] * lk2_ref[...], axis=1, keepdims=True)
    return jnp.exp(a) - jnp.exp(b) + lam_init


def _subln(o, g, lam_init):
    o = o * lax.rsqrt(jnp.mean(jnp.square(o), axis=-1, keepdims=True) + SUBLN_EPS)
    return o * g * (1.0 - lam_init)


DIFF_TQ = 256


def _diff_prompt_kernel(q_ref, k_ref, v_ref, lq1_ref, lk1_ref, lq2_ref, lk2_ref, g_ref, o_ref,
                        m_ref, l_ref, acc_ref, *, n_blocks, lam_init):
    qi = pl.program_id(2)
    row0 = pl.multiple_of(qi * DIFF_TQ, DIFF_TQ)
    vb0 = v_ref[pl.ds(row0, DIFF_TQ), :].astype(BF16)
    qbs = []
    for c in range(2):
        cs = slice(c * HEAD_DIM, (c + 1) * HEAD_DIM)
        qb = q_ref[:, cs].astype(BF16)
        qbs.append(qb)
        kb = k_ref[pl.ds(row0, DIFF_TQ), cs].astype(BF16)
        s = _nt_dot(qb, kb) * ATTN_SCALE
        r_id = lax.broadcasted_iota(jnp.int32, s.shape, 0)
        c_id = lax.broadcasted_iota(jnp.int32, s.shape, 1)
        s = jnp.where(c_id <= r_id, s, -jnp.inf)
        m0 = jnp.max(s, axis=1, keepdims=True)
        p = jnp.exp(s - m0)
        m_ref[c] = m0
        l_ref[c] = jnp.sum(p, axis=1, keepdims=True)
        acc_ref[c] = jnp.dot(p.astype(BF16), vb0, preferred_element_type=F32)

    for j in range(n_blocks - 1):
        @pl.when(j < qi)
        def _(j=j):
            vj = v_ref[j * DIFF_TQ:(j + 1) * DIFF_TQ, :].astype(BF16)
            for c in range(2):
                cs = slice(c * HEAD_DIM, (c + 1) * HEAD_DIM)
                kj = k_ref[j * DIFF_TQ:(j + 1) * DIFF_TQ, cs].astype(BF16)
                sj = _nt_dot(qbs[c], kj) * ATTN_SCALE
                _flash_update(sj, vj, m_ref.at[c], l_ref.at[c], acc_ref.at[c])

    lam = _lambda_value(lq1_ref, lk1_ref, lq2_ref, lk2_ref, lam_init)
    o = acc_ref[0] / l_ref[0] - lam * (acc_ref[1] / l_ref[1])
    o_ref[...] = _subln(o, g_ref[...], lam_init).astype(o_ref.dtype)


def _diff_prompt(q, k, v, lams, g, batch, seq, lam_init):
    nq = seq // DIFF_TQ
    vec = pl.BlockSpec((1, HEAD_DIM), lambda b, h, i: (0, 0))
    return pl.pallas_call(
        functools.partial(_diff_prompt_kernel, n_blocks=nq, lam_init=lam_init),
        grid=(batch, DIFF_HEADS, nq),
        in_specs=[
            pl.BlockSpec((DIFF_TQ, DIFF_V_DIM), lambda b, h, i: (b * nq + i, h)),
            pl.BlockSpec((seq, DIFF_V_DIM), lambda b, h, i: (b, h)),
            pl.BlockSpec((seq, DIFF_V_DIM), lambda b, h, i: (b, h)),
            vec, vec, vec, vec,
            pl.BlockSpec((1, DIFF_V_DIM), lambda b, h, i: (0, 0)),
        ],
        out_specs=pl.BlockSpec((DIFF_TQ, DIFF_V_DIM), lambda b, h, i: (b * nq + i, h)),
        out_shape=jax.ShapeDtypeStruct((batch * seq, DIFF_HEADS * DIFF_V_DIM), BF16),
        scratch_shapes=[pltpu.VMEM((2, DIFF_TQ, 1), F32), pltpu.VMEM((2, DIFF_TQ, 1), F32),
                        pltpu.VMEM((2, DIFF_TQ, DIFF_V_DIM), F32)],
        compiler_params=_cparams(3),
        name="diff_prompt",
    )(q, k, v, *lams, g)


def _layer_norm_rows(z, g, b):
    mu = jnp.mean(z, axis=-1, keepdims=True)
    zc = z - mu
    var = jnp.mean(zc * zc, axis=-1, keepdims=True)
    return zc * lax.rsqrt(var + LN_EPS) * g + b


def _route(logits):
    lane = lax.broadcasted_iota(jnp.int32, logits.shape, 1)
    lane_f = lane.astype(F32)
    big = float(logits.shape[1])
    lg = jnp.where(lane < N_GROUPS, logits, -jnp.inf)
    mg = jnp.max(lg, axis=1, keepdims=True)
    sg = jnp.sum(jnp.exp(lg - mg), axis=1, keepdims=True)
    g_val = 1.0 / sg
    g_idx = jnp.min(jnp.where(lg == mg, lane_f, big), axis=1, keepdims=True)
    lo = N_GROUPS + EXPERTS_PER_GROUP * g_idx
    in_grp = jnp.where(lane_f >= lo, jnp.where(lane_f < lo + EXPERTS_PER_GROUP, 1.0, 0.0), 0.0)
    le = jnp.where(in_grp > 0.0, logits, -jnp.inf)
    me = jnp.max(le, axis=1, keepdims=True)
    se = jnp.sum(jnp.exp(le - me), axis=1, keepdims=True)
    e1 = jnp.min(jnp.where(le == me, lane_f, big), axis=1, keepdims=True)
    le2 = jnp.where(lane_f == e1, -jnp.inf, le)
    m2 = jnp.max(le2, axis=1, keepdims=True)
    e2 = jnp.min(jnp.where(le2 == m2, lane_f, big), axis=1, keepdims=True)
    p1 = 1.0 / se
    p2 = jnp.exp(m2 - me) / se
    w1 = g_val * p1 / (p1 + p2)
    w2 = g_val * p2 / (p1 + p2)
    out = jnp.where(lane == 0, e1 - N_GROUPS,
                    jnp.where(lane == 1, e2 - N_GROUPS,
                              jnp.where(lane == 2, w1, jnp.where(lane == 3, w2, 0.0))))
    return out


def _ln_route_kernel(x_ref, a_ref, g_ref, b_ref, wr_ref, br_ref, y_ref, r_ref):
    y = _layer_norm_rows(DEEPNORM_ALPHA * x_ref[...] + a_ref[...], g_ref[...], b_ref[...])
    y_ref[...] = y
    logits = jnp.dot(y, wr_ref[...], precision=HIGHEST, preferred_element_type=F32) + br_ref[...]
    r_ref[...] = _route(logits)


def _ln_route(x, a, g, b, w_router, b_router, tm):
    m, d = x.shape
    row = pl.BlockSpec((tm, d), lambda i: (i, 0))
    vec = pl.BlockSpec((1, d), lambda i: (0, 0))
    return pl.pallas_call(
        _ln_route_kernel,
        grid=(m // tm,),
        in_specs=[row, row, vec, vec,
                  pl.BlockSpec((d, 128), lambda i: (0, 0)),
                  pl.BlockSpec((1, 128), lambda i: (0, 0))],
        out_specs=[row, pl.BlockSpec((tm, 128), lambda i: (i, 0))],
        out_shape=[jax.ShapeDtypeStruct((m, d), F32), jax.ShapeDtypeStruct((m, 128), F32)],
        compiler_params=_cparams(1),
        name="ln_route",
    )(x, a, g, b, w_router, b_router)


GATHER_ROWS = 128


def _gather_rows_kernel(tok_ref, src_ref, dst_ref, sem):
    base = pl.program_id(0) * GATHER_ROWS

    def copy(r):
        return pltpu.make_async_copy(src_ref.at[pl.ds(tok_ref[base + r], 1)],
                                     dst_ref.at[pl.ds(base + r, 1)], sem)

    def start(r, c):
        copy(r).start()
        return c

    def wait(r, c):
        copy(r).wait()
        return c

    lax.fori_loop(0, GATHER_ROWS, start, 0)
    lax.fori_loop(0, GATHER_ROWS, wait, 0)


def _gather_rows(src, row_token, r_max):
    return pl.pallas_call(
        _gather_rows_kernel,
        grid_spec=pltpu.PrefetchScalarGridSpec(
            num_scalar_prefetch=1,
            grid=(r_max // GATHER_ROWS,),
            in_specs=[pl.BlockSpec(memory_space=pl.ANY)],
            out_specs=pl.BlockSpec(memory_space=pl.ANY),
            scratch_shapes=[pltpu.SemaphoreType.DMA(())],
        ),
        out_shape=jax.ShapeDtypeStruct((r_max, src.shape[1]), src.dtype),
        compiler_params=pltpu.CompilerParams(dimension_semantics=("arbitrary",), has_side_effects=True),
        name="moe_gather",
    )(row_token, src)


def _expert_changed(te_ref, i):
    prev = te_ref[jnp.maximum(i - 1, 0)]
    return jnp.logical_or(i == 0, te_ref[i] != prev)


def _moe_up_kernel(te_ref, nt_ref, x_ref, wg_ref, wu_ref, gate_ref, h_ref, wgb_ref, wub_ref):
    i = pl.program_id(1)

    @pl.when(_expert_changed(te_ref, i))
    def _():
        wgb_ref[...] = wg_ref[...].astype(BF16)
        wub_ref[...] = wu_ref[...].astype(BF16)

    @pl.when(i < nt_ref[0])
    def _():
        xb = x_ref[...].astype(BF16)
        a = jnp.dot(xb, wgb_ref[...], preferred_element_type=F32)
        u = jnp.dot(xb, wub_ref[...], preferred_element_type=F32)
        h_ref[...] = (a * jax.nn.sigmoid(a) * u * gate_ref[...]).astype(h_ref.dtype)

    @pl.when(i >= nt_ref[0])
    def _():
        h_ref[...] = jnp.zeros(h_ref.shape, h_ref.dtype)


def _moe_up(x_sorted, w_gate, w_up, row_gate, tile_expert, n_tiles, layer, tf):
    r_max, d = x_sorted.shape
    t_max = r_max // ROW_TILE
    wspec = pl.BlockSpec((None, None, d, tf), lambda f, i, te, nt: (layer, te[i], 0, f))
    return pl.pallas_call(
        _moe_up_kernel,
        grid_spec=pltpu.PrefetchScalarGridSpec(
            num_scalar_prefetch=2,
            grid=(EXPERT_FF // tf, t_max),
            in_specs=[
                pl.BlockSpec((ROW_TILE, d), lambda f, i, te, nt: (i, 0)),
                wspec, wspec,
                pl.BlockSpec((ROW_TILE, 1), lambda f, i, te, nt: (i, 0)),
            ],
            out_specs=pl.BlockSpec((ROW_TILE, tf), lambda f, i, te, nt: (i, f)),
            scratch_shapes=[pltpu.VMEM((d, tf), BF16), pltpu.VMEM((d, tf), BF16)],
        ),
        out_shape=jax.ShapeDtypeStruct((r_max, EXPERT_FF), BF16),
        compiler_params=_cparams(2),
        name="moe_up",
    )(tile_expert, n_tiles, x_sorted, w_gate, w_up, row_gate)


def _moe_down_kernel(te_ref, nt_ref, h_ref, wd_ref, y_ref, wdb_ref):
    i = pl.program_id(1)

    @pl.when(_expert_changed(te_ref, i))
    def _():
        wdb_ref[...] = wd_ref[...].astype(BF16)

    @pl.when(i < nt_ref[0])
    def _():
        y_ref[...] = jnp.dot(h_ref[...], wdb_ref[...], preferred_element_type=F32)

    @pl.when(i >= nt_ref[0])
    def _():
        y_ref[...] = jnp.zeros(y_ref.shape, y_ref.dtype)


def _moe_down(h, w_down, tile_expert, n_tiles, layer, tn):
    r_max, ff = h.shape
    d = w_down.shape[-1]
    t_max = r_max // ROW_TILE
    return pl.pallas_call(
        _moe_down_kernel,
        grid_spec=pltpu.PrefetchScalarGridSpec(
            num_scalar_prefetch=2,
            grid=(d // tn, t_max),
            in_specs=[
                pl.BlockSpec((ROW_TILE, ff), lambda n, i, te, nt: (i, 0)),
                pl.BlockSpec((None, None, ff, tn), lambda n, i, te, nt: (layer, te[i], 0, n)),
            ],
            out_specs=pl.BlockSpec((ROW_TILE, tn), lambda n, i, te, nt: (i, n)),
            scratch_shapes=[pltpu.VMEM((ff, tn), BF16)],
        ),
        out_shape=jax.ShapeDtypeStruct((r_max, d), F32),
        compiler_params=_cparams(2),
        name="moe_down",
    )(tile_expert, n_tiles, h, w_down)


COMBINE_ROWS = 128


def _combine_ln_kernel(pos_ref, x_ref, g_ref, b_ref, ys_ref, y_ref, yb_ref, buf_ref, sem):
    base = pl.program_id(0) * COMBINE_ROWS

    def copy(r, k):
        return pltpu.make_async_copy(ys_ref.at[pl.ds(pos_ref[2 * (base + r) + k], 1)],
                                     buf_ref.at[k, pl.ds(r, 1)], sem)

    def start(r, c):
        copy(r, 0).start()
        copy(r, 1).start()
        return c

    def wait(r, c):
        copy(r, 0).wait()
        copy(r, 1).wait()
        return c

    lax.fori_loop(0, COMBINE_ROWS, start, 0)
    lax.fori_loop(0, COMBINE_ROWS, wait, 0)
    f = buf_ref[0] + buf_ref[1]
    y = _layer_norm_rows(DEEPNORM_ALPHA * x_ref[...] + f, g_ref[...], b_ref[...])
    y_ref[...] = y
    yb_ref[...] = y.astype(BF16)


def _combine_ln(x, y_sorted, pos, g, b):
    m, d = x.shape
    row = pl.BlockSpec((COMBINE_ROWS, d), lambda i, pos: (i, 0))
    vec = pl.BlockSpec((1, d), lambda i, pos: (0, 0))
    return pl.pallas_call(
        _combine_ln_kernel,
        grid_spec=pltpu.PrefetchScalarGridSpec(
            num_scalar_prefetch=1,
            grid=(m // COMBINE_ROWS,),
            in_specs=[row, vec, vec, pl.BlockSpec(memory_space=pl.ANY)],
            out_specs=[row, row],
            scratch_shapes=[pltpu.VMEM((2, COMBINE_ROWS, d), F32), pltpu.SemaphoreType.DMA(())],
        ),
        out_shape=[jax.ShapeDtypeStruct((m, d), F32), jax.ShapeDtypeStruct((m, d), BF16)],
        compiler_params=_cparams(1),
        name="moe_combine_ln",
    )(pos, x, g, b, y_sorted)


def _dispatch(route, n_valid, r_max):
    e = route[:n_valid, 0:2].astype(jnp.int32).reshape(-1)
    w = route[:n_valid, 2:4].reshape(-1)
    onehot = (e[:, None] == jnp.arange(N_EXPERTS, dtype=jnp.int32)[None, :]).astype(jnp.int32)
    csum = jnp.cumsum(onehot, axis=0)
    counts = csum[-1]
    rank = jnp.sum(csum * onehot, axis=1) - 1
    tiles_per = (counts + ROW_TILE - 1) // ROW_TILE
    tile_end = jnp.cumsum(tiles_per)
    tile_start = tile_end - tiles_per
    pos = tile_start[e] * ROW_TILE + rank
    n_tiles = tile_end[-1]
    t_max = r_max // ROW_TILE
    tile_ids = jnp.arange(t_max, dtype=jnp.int32)
    tile_expert = jnp.sum((tile_ids[:, None] >= tile_end[None, :]).astype(jnp.int32), axis=1)
    last_expert = jnp.sum((n_tiles - 1 >= tile_end).astype(jnp.int32))
    tile_expert = jnp.where(tile_ids < n_tiles, tile_expert, last_expert).astype(jnp.int32)
    row_token = jnp.zeros((r_max,), jnp.int32).at[pos].set(jnp.arange(2 * n_valid, dtype=jnp.int32) // 2)
    row_gate = jnp.zeros((r_max,), F32).at[pos].set(w)
    return pos.astype(jnp.int32), row_token, row_gate[:, None], tile_expert, n_tiles.astype(jnp.int32)


KMEAN_PAGES = 8
PAGES_PER_BLOCK = MOBA_BLOCK // PAGE_SIZE


def _kmean_kernel(pt_ref, *refs):
    pages, o_ref = refs[:-1], refs[-1]
    for blk in range(KMEAN_PAGES // PAGES_PER_BLOCK):
        tot = jnp.sum(pages[2 * blk][...], axis=0) + jnp.sum(pages[2 * blk + 1][...], axis=0)
        o_ref[blk] = tot * (1.0 / MOBA_BLOCK)


def _kmean(cache_k, page_table, layer):
    db, n_pages = page_table.shape
    steps = n_pages // KMEAN_PAGES
    bps = KMEAN_PAGES // PAGES_PER_BLOCK

    def page_spec(kk):
        return pl.BlockSpec((None, None, PAGE_SIZE, MOBA_HEADS, HEAD_DIM),
                            lambda b, s, pt: (layer, pt[b, s * KMEAN_PAGES + kk], 0, 0, 0))

    return pl.pallas_call(
        _kmean_kernel,
        grid_spec=pltpu.PrefetchScalarGridSpec(
            num_scalar_prefetch=1,
            grid=(db, steps),
            in_specs=[page_spec(kk) for kk in range(KMEAN_PAGES)],
            out_specs=pl.BlockSpec((None, bps, MOBA_HEADS, HEAD_DIM), lambda b, s, pt: (b, s, 0, 0)),
        ),
        out_shape=jax.ShapeDtypeStruct((db, n_pages // PAGES_PER_BLOCK, MOBA_HEADS, HEAD_DIM), F32),
        compiler_params=_cparams(2),
        name="moba_sample_kmean",
    )(page_table, *([cache_k] * KMEAN_PAGES))


def _topk_kernel(km_ref, q_ref, idx_ref, keep_ref):
    gate = jnp.sum(km_ref[...] * q_ref[...][None], axis=-1, keepdims=True)
    nb = gate.shape[0]
    bid = lax.broadcasted_iota(jnp.int32, gate.shape, 0)
    for r in range(MOBA_TOPK):
        m = jnp.max(gate, axis=0, keepdims=True)
        first = jnp.min(jnp.where(gate == m, bid, nb), axis=0, keepdims=True)
        idx_ref[r] = first[0]
        keep_ref[r] = jnp.where(jnp.abs(m[0]) < jnp.inf, 1, 0)
        gate = jnp.where(bid == first, -jnp.inf, gate)


def _sample_topk(k_mean, q):
    db, nb = k_mean.shape[:2]
    out = jax.ShapeDtypeStruct((db, MOBA_TOPK, MOBA_HEADS, 1), jnp.int32)
    ospec = pl.BlockSpec((None, MOBA_TOPK, MOBA_HEADS, 1), lambda b: (b, 0, 0, 0))
    return pl.pallas_call(
        _topk_kernel,
        grid=(db,),
        in_specs=[pl.BlockSpec((None, nb, MOBA_HEADS, HEAD_DIM), lambda b: (b, 0, 0, 0)),
                  pl.BlockSpec((None, MOBA_HEADS, HEAD_DIM), lambda b: (b, 0, 0))],
        out_specs=[ospec, ospec],
        out_shape=[out, out],
        compiler_params=_cparams(1),
        name="moba_sample_topk",
    )(k_mean, q)


SEL_ROWS = MOBA_TOPK * MOBA_BLOCK


def _moba_sample_kernel(page_ref, keep_ref, q_ref, kn_ref, vn_ref, ck_ref, cv_ref, o_ref,
                        kbuf, vbuf, sem, *, layer):
    b = pl.program_id(0)
    n_sel_pages = MOBA_TOPK * PAGES_PER_BLOCK

    def copies(h, j):
        page = page_ref[(b * MOBA_HEADS + h) * n_sel_pages + j]
        dst = pl.ds(j * PAGE_SIZE, PAGE_SIZE)
        return (pltpu.make_async_copy(ck_ref.at[layer, page, :, h, :], kbuf.at[h, dst, :], sem.at[0]),
                pltpu.make_async_copy(cv_ref.at[layer, page, :, h, :], vbuf.at[h, dst, :], sem.at[1]))

    def start(t, c):
        ck, cv = copies(t // n_sel_pages, t % n_sel_pages)
        ck.start()
        cv.start()
        return c

    def wait(t, c):
        ck, cv = copies(t // n_sel_pages, t % n_sel_pages)
        ck.wait()
        cv.wait()
        return c

    lax.fori_loop(0, MOBA_HEADS * n_sel_pages, start, 0)
    lax.fori_loop(0, MOBA_HEADS * n_sel_pages, wait, 0)

    q = q_ref[...]
    s = jnp.sum(kbuf[...] * q[:, None, :], axis=-1, keepdims=True) * ATTN_SCALE
    rid = lax.broadcasted_iota(jnp.int32, s.shape, 1)
    keep = keep_ref[...]
    kept = jnp.zeros(s.shape, jnp.int32)
    for r in range(MOBA_TOPK):
        kept = jnp.where(rid // MOBA_BLOCK == r, keep[r][:, None, :], kept)
    s = jnp.where(kept > 0, s, -jnp.inf)
    s_new = jnp.sum(kn_ref[...] * q, axis=-1, keepdims=True)[:, None, :] * ATTN_SCALE
    m = jnp.maximum(jnp.max(s, axis=1, keepdims=True), s_new)
    p = jnp.exp(s - m)
    p_new = jnp.exp(s_new - m)
    denom = jnp.sum(p, axis=1, keepdims=True) + p_new
    num = jnp.sum(p * vbuf[...], axis=1, keepdims=True) + p_new * vn_ref[...][:, None, :]
    o_ref[...] = (num / denom)[:, 0, :]


def _moba_sample(q, k_new, v_new, cache_k, cache_v, sel_pages, keep, layer):
    db = q.shape[0]
    hd = pl.BlockSpec((None, MOBA_HEADS, HEAD_DIM), lambda b, pg: (b, 0, 0))
    return pl.pallas_call(
        functools.partial(_moba_sample_kernel, layer=layer),
        grid_spec=pltpu.PrefetchScalarGridSpec(
            num_scalar_prefetch=1,
            grid=(db,),
            in_specs=[pl.BlockSpec((None, MOBA_TOPK, MOBA_HEADS, 1), lambda b, pg: (b, 0, 0, 0)),
                      hd, hd, hd,
                      pl.BlockSpec(memory_space=pl.ANY), pl.BlockSpec(memory_space=pl.ANY)],
            out_specs=hd,
            scratch_shapes=[pltpu.VMEM((MOBA_HEADS, SEL_ROWS, HEAD_DIM), F32),
                            pltpu.VMEM((MOBA_HEADS, SEL_ROWS, HEAD_DIM), F32),
                            pltpu.SemaphoreType.DMA((2,))],
        ),
        out_shape=jax.ShapeDtypeStruct((db, MOBA_HEADS, HEAD_DIM), F32),
        compiler_params=_cparams(1),
        name="moba_sample_attend",
    )(sel_pages, keep, q, k_new, v_new, cache_k, cache_v)


DIFF_PAGES = 4


def _diff_sample_kernel(pt_ref, *refs, lam_init):
    kp = refs[:DIFF_PAGES]
    vp = refs[DIFF_PAGES:2 * DIFF_PAGES]
    (q_ref, kn_ref, vn_ref, lq1_ref, lk1_ref, lq2_ref, lk2_ref, g_ref, o_ref,
     m_ref, l_ref, acc_ref) = refs[2 * DIFF_PAGES:]
    step = pl.program_id(1)

    @pl.when(step == 0)
    def _():
        for c in range(2):
            s_new = jnp.sum(kn_ref[c] * q_ref[c], axis=-1, keepdims=True) * ATTN_SCALE
            m_ref[c] = s_new
            l_ref[c] = jnp.ones(s_new.shape, F32)
            acc_ref[c] = vn_ref[...]

    for c in range(2):
        qc = q_ref[c]
        s = jnp.concatenate(
            [jnp.sum(kp[p][:, pl.ds(c, DIFF_HEADS, stride=2), :] * qc[None], axis=-1, keepdims=True)
             for p in range(DIFF_PAGES)], axis=0) * ATTN_SCALE
        m_old = m_ref[c]
        m_new = jnp.maximum(m_old, jnp.max(s, axis=0))
        alpha = jnp.exp(m_old - m_new)
        pr = jnp.exp(s - m_new[None])
        l_ref[c] = alpha * l_ref[c] + jnp.sum(pr, axis=0)
        pv = jnp.zeros(acc_ref.shape[1:], F32)
        for p in range(DIFF_PAGES):
            pv = pv + jnp.sum(pr[p * PAGE_SIZE:(p + 1) * PAGE_SIZE] * vp[p][...], axis=0)
        acc_ref[c] = alpha * acc_ref[c] + pv
        m_ref[c] = m_new

    @pl.when(step == pl.num_programs(1) - 1)
    def _():
        lam = _lambda_value(lq1_ref, lk1_ref, lq2_ref, lk2_ref, lam_init)
        o = acc_ref[0] / l_ref[0] - lam * (acc_ref[1] / l_ref[1])
        o_ref[...] = _subln(o, g_ref[...], lam_init)


def _diff_sample(q, k_new, v_new, cache_k, cache_v, page_table, lams, g, layer, lam_init):
    db, n_pages = page_table.shape
    steps = n_pages // DIFF_PAGES

    def kspec(kk):
        return pl.BlockSpec((None, None, PAGE_SIZE, 2 * DIFF_HEADS, HEAD_DIM),
                            lambda b, s, pt: (layer, pt[b, s * DIFF_PAGES + kk], 0, 0, 0))

    def vspec(kk):
        return pl.BlockSpec((None, None, PAGE_SIZE, DIFF_HEADS, DIFF_V_DIM),
                            lambda b, s, pt: (layer, pt[b, s * DIFF_PAGES + kk], 0, 0, 0))

    qk = pl.BlockSpec((None, 2, DIFF_HEADS, HEAD_DIM), lambda b, s, pt: (b, 0, 0, 0))
    vo = pl.BlockSpec((None, DIFF_HEADS, DIFF_V_DIM), lambda b, s, pt: (b, 0, 0))
    vec = pl.BlockSpec((1, HEAD_DIM), lambda b, s, pt: (0, 0))
    return pl.pallas_call(
        functools.partial(_diff_sample_kernel, lam_init=lam_init),
        grid_spec=pltpu.PrefetchScalarGridSpec(
            num_scalar_prefetch=1,
            grid=(db, steps),
            in_specs=([kspec(kk) for kk in range(DIFF_PAGES)] + [vspec(kk) for kk in range(DIFF_PAGES)]
                      + [qk, qk, vo, vec, vec, vec, vec,
                         pl.BlockSpec((1, DIFF_V_DIM), lambda b, s, pt: (0, 0))]),
            out_specs=vo,
            scratch_shapes=[pltpu.VMEM((2, DIFF_HEADS, 1), F32), pltpu.VMEM((2, DIFF_HEADS, 1), F32),
                            pltpu.VMEM((2, DIFF_HEADS, DIFF_V_DIM), F32)],
        ),
        out_shape=jax.ShapeDtypeStruct((db, DIFF_HEADS, DIFF_V_DIM), F32),
        compiler_params=_cparams(2),
        name="diff_sample",
    )(page_table, *([cache_k] * DIFF_PAGES), *([cache_v] * DIFF_PAGES), q, k_new, v_new, *lams, g)


def _rope_tables(positions):
    half = HEAD_DIM // 2
    inv_freq = ROPE_THETA ** (-jnp.arange(half, dtype=F32) / half)
    ang = positions.astype(F32)[:, None] * inv_freq[None, :]
    cos, sin = jnp.cos(ang), jnp.sin(ang)
    return jnp.concatenate([cos, cos], axis=-1), jnp.concatenate([-sin, sin], axis=-1)


def kernel(x_prompt, x_sample, cache_moba_k, cache_moba_v, cache_diff_k, cache_diff_v, page_table, w_in, w_out, lambda_q1, lambda_k1, lambda_q2, lambda_k2, subln_g, ln1_g, ln1_b, w_router_group, b_router_group, w_router_expert, b_router_expert, w_exp_gate, w_exp_up, w_exp_down, ln2_g, ln2_b):
    batch, seq, d = x_prompt.shape
    db = x_sample.shape[0]
    depth = w_in.shape[0]
    n_pages = page_table.shape[1]
    past = n_pages * PAGE_SIZE
    m_prompt = batch * seq
    n_valid = m_prompt + db
    m_pad = -(-n_valid // (3 * ROW_TILE)) * (3 * ROW_TILE)
    tm_big = 3 * ROW_TILE
    r_max = ((2 * n_valid + N_EXPERTS * (ROW_TILE - 1)) // ROW_TILE + 1) * ROW_TILE

    x = jnp.concatenate([x_prompt.reshape(m_prompt, d), x_sample.reshape(db, d),
                         jnp.zeros((m_pad - n_valid, d), F32)], axis=0)
    xb = x.astype(BF16)
    positions = jnp.concatenate([jnp.tile(jnp.arange(seq, dtype=jnp.int32), batch),
                                 jnp.full((db,), past, jnp.int32),
                                 jnp.zeros((m_pad - n_valid,), jnp.int32)])
    cos, sin = _rope_tables(positions)

    cdk = cache_diff_k.reshape(cache_diff_k.shape[:3] + (2 * DIFF_HEADS, HEAD_DIM))

    outs = [[] for _ in range(8)]
    for l in range(depth):
        lam_init = 0.8 - 0.6 * math.exp(-0.3 * l)
        lams = (lambda_q1[l][None], lambda_k1[l][None], lambda_q2[l][None], lambda_k2[l][None])
        g_sub = subln_g[l][None]

        proj = functools.partial(_project, xb, w_in, l, cos=cos, sin=sin, tm=tm_big, tn=512)
        qm = proj(0 * GROUP_WIDTH, GROUP_WIDTH, rope=True)
        km = proj(1 * GROUP_WIDTH, GROUP_WIDTH, rope=True)
        vm = proj(2 * GROUP_WIDTH, GROUP_WIDTH, rope=False)
        qd = proj(3 * GROUP_WIDTH, GROUP_WIDTH, rope=True)
        kd = proj(4 * GROUP_WIDTH, GROUP_WIDTH, rope=True)
        vd = proj(5 * GROUP_WIDTH, GROUP_WIDTH, rope=False)

        om_p = _moba_prompt(qm, km, vm, batch, seq)
        od_p = _diff_prompt(qd, kd, vd, lams, g_sub, batch, seq, lam_init)

        sl = slice(m_prompt, n_valid)
        qm_s = qm[sl].reshape(db, MOBA_HEADS, HEAD_DIM)
        km_s = km[sl].reshape(db, MOBA_HEADS, HEAD_DIM)
        vm_s = vm[sl].reshape(db, MOBA_HEADS, HEAD_DIM)
        k_mean = _kmean(cache_moba_k, page_table, l)
        sel_idx, keep = _sample_topk(k_mean, qm_s)
        blk = jnp.transpose(sel_idx[..., 0], (0, 2, 1))
        logical = blk[..., None] * PAGES_PER_BLOCK + jnp.arange(PAGES_PER_BLOCK, dtype=jnp.int32)
        phys = jnp.take_along_axis(page_table[:, None, :], logical.reshape(db, MOBA_HEADS, -1), axis=2)
        om_s = _moba_sample(qm_s, km_s, vm_s, cache_moba_k, cache_moba_v,
                            phys.reshape(-1).astype(jnp.int32), keep, l)

        qd_s = jnp.transpose(qd[sl].reshape(db, DIFF_HEADS, 2, HEAD_DIM), (0, 2, 1, 3))
        kd_s = jnp.transpose(kd[sl].reshape(db, DIFF_HEADS, 2, HEAD_DIM), (0, 2, 1, 3))
        vd_s = vd[sl].reshape(db, DIFF_HEADS, DIFF_V_DIM)
        od_s = _diff_sample(qd_s, kd_s, vd_s, cdk, cache_diff_v, page_table, lams, g_sub, l, lam_init)

        cat = jnp.concatenate([
            jnp.concatenate([om_p, od_p], axis=1),
            jnp.concatenate([om_s.reshape(db, -1), od_s.reshape(db, -1)], axis=1).astype(BF16),
            jnp.zeros((m_pad - n_valid, d), BF16)], axis=0)
        a = _project(cat, w_out, l, 0, d, cos, sin, rope=False, tm=tm_big, tn=512)

        w_router = jnp.concatenate([w_router_group[l], w_router_expert[l],
                                    jnp.zeros((d, 128 - N_GROUPS - N_EXPERTS), F32)], axis=1)
        b_router = jnp.concatenate([b_router_group[l], b_router_expert[l],
                                    jnp.zeros((128 - N_GROUPS - N_EXPERTS,), F32)])[None]
        h1, route = _ln_route(x, a, ln1_g[l][None], ln1_b[l][None], w_router, b_router, ROW_TILE)

        pos, row_token, row_gate, tile_expert, n_tiles = _dispatch(route, n_valid, r_max)
        x_sorted = _gather_rows(h1, row_token, r_max)
        hid = _moe_up(x_sorted, w_exp_gate, w_exp_up, row_gate, tile_expert, n_tiles[None], l, 512)
        y_sorted = _moe_down(hid, w_exp_down, tile_expert, n_tiles[None], l, 2048)
        pos_pad = jnp.concatenate([pos, jnp.zeros((2 * (m_pad - n_valid),), jnp.int32)])
        x, xb = _combine_ln(h1, y_sorted, pos_pad, ln2_g[l][None], ln2_b[l][None])

        outs[0].append(km[:m_prompt].reshape(batch, seq, MOBA_HEADS, HEAD_DIM))
        outs[1].append(vm[:m_prompt].reshape(batch, seq, MOBA_HEADS, HEAD_DIM))
        outs[2].append(kd[:m_prompt].reshape(batch, seq, DIFF_HEADS, 2, HEAD_DIM))
        outs[3].append(vd[:m_prompt].reshape(batch, seq, DIFF_HEADS, DIFF_V_DIM))
        outs[4].append(km[sl].reshape(db, 1, MOBA_HEADS, HEAD_DIM))
        outs[5].append(vm[sl].reshape(db, 1, MOBA_HEADS, HEAD_DIM))
        outs[6].append(kd[sl].reshape(db, 1, DIFF_HEADS, 2, HEAD_DIM))
        outs[7].append(vd[sl].reshape(db, 1, DIFF_HEADS, DIFF_V_DIM))

    y_prompt = x[:m_prompt].reshape(batch, seq, d)
    y_sample = x[m_prompt:n_valid].reshape(db, 1, d)
    return (y_prompt, y_sample) + tuple(jnp.stack(o) for o in outs)
```

```python
import functools
import math

import jax
import jax.numpy as jnp
from jax import lax
from jax.experimental import pallas as pl
from jax.experimental.pallas import tpu as pltpu

F32 = jnp.float32
BF16 = jnp.bfloat16
HIGHEST = lax.Precision.HIGHEST

D_MODEL = 4096
HEAD_DIM = 128
MOBA_HEADS = 16
DIFF_HEADS = 8
DIFF_V_DIM = 256
GROUP_WIDTH = 2048
MOBA_BLOCK = 256
MOBA_TOPK = 3
PAGE_SIZE = 128
ROPE_THETA = 10000.0
N_GROUPS = 4
EXPERTS_PER_GROUP = 8
N_EXPERTS = 32
EXPERT_FF = 1024
DEPTH = 2
DEEPNORM_ALPHA = (2 * DEPTH) ** 0.25
LN_EPS = 1e-5
SUBLN_EPS = 1e-5
ATTN_SCALE = HEAD_DIM ** -0.5

ROW_TILE = 256
VMEM_LIMIT = 56 * 1024 * 1024


def _cparams(n_axes):
    return pltpu.CompilerParams(dimension_semantics=("arbitrary",) * n_axes,
                                vmem_limit_bytes=VMEM_LIMIT)


def _proj_kernel(x_ref, w_ref, cos_ref, sin_ref, o_ref, wb_ref, *, rope):
    @pl.when(pl.program_id(1) == 0)
    def _():
        wb_ref[...] = w_ref[...].astype(BF16)

    acc = jnp.dot(x_ref[...], wb_ref[...], preferred_element_type=F32)
    if rope:
        cos = cos_ref[...]
        sin = sin_ref[...]
        for h in range(acc.shape[1] // HEAD_DIM):
            z = acc[:, h * HEAD_DIM:(h + 1) * HEAD_DIM]
            o_ref[:, h * HEAD_DIM:(h + 1) * HEAD_DIM] = (
                z * cos + pltpu.roll(z, HEAD_DIM // 2, 1) * sin)
    else:
        o_ref[...] = acc


def _project(x_bf16, w, layer, col0, n_cols, cos, sin, *, rope, tm, tn):
    m, k = x_bf16.shape
    jb0 = col0 // tn
    return pl.pallas_call(
        functools.partial(_proj_kernel, rope=rope),
        grid=(n_cols // tn, m // tm),
        in_specs=[
            pl.BlockSpec((tm, k), lambda j, i: (i, 0)),
            pl.BlockSpec((None, k, tn), lambda j, i: (layer, 0, jb0 + j)),
            pl.BlockSpec((tm, HEAD_DIM), lambda j, i: (i, 0)),
            pl.BlockSpec((tm, HEAD_DIM), lambda j, i: (i, 0)),
        ],
        out_specs=pl.BlockSpec((tm, tn), lambda j, i: (i, j)),
        out_shape=jax.ShapeDtypeStruct((m, n_cols), F32),
        scratch_shapes=[pltpu.VMEM((k, tn), BF16)],
        compiler_params=_cparams(2),
        name="proj_rope" if rope else "proj",
    )(x_bf16, w, cos, sin)


def _nt_dot(a, b, **kw):
    return lax.dot_general(a, b, (((1,), (1,)), ((), ())), preferred_element_type=F32, **kw)


def _causal_mask(n):
    r_id = lax.broadcasted_iota(jnp.int32, (n, n), 0)
    c_id = lax.broadcasted_iota(jnp.int32, (n, n), 1)
    return c_id <= r_id


def _softmax_parts(s):
    m = jnp.max(s, axis=1, keepdims=True)
    p = jnp.exp(s - m)
    return p, jnp.sum(p, axis=1, keepdims=True)


def _moba_select(gate, qi):
    blk = lax.broadcasted_iota(jnp.int32, gate.shape, 1)
    gate = jnp.where(blk < qi, gate, -jnp.inf)
    finite = jnp.where(jnp.abs(gate) < jnp.inf, 1.0, 0.0)
    if qi <= MOBA_TOPK:
        return finite
    rank = jnp.zeros(gate.shape, jnp.int32)
    for n2 in range(qi):
        col = gate[:, n2:n2 + 1]
        rank = rank + jnp.where(col > gate, 1, jnp.where(col == gate, jnp.where(n2 < blk, 1, 0), 0))
    return jnp.where(rank < MOBA_TOPK, finite, 0.0)


def _moba_prompt_kernel(q_ref, k_ref, v_ref, o_ref, *, n_blocks):
    k = k_ref[...]
    kb = k.astype(BF16)
    vb = v_ref[...].astype(BF16)
    k_mean = jnp.mean(k.reshape(n_blocks, MOBA_BLOCK, HEAD_DIM), axis=1)
    gate_all = _nt_dot(q_ref[...], k_mean, precision=HIGHEST)
    causal = _causal_mask(MOBA_BLOCK)
    for qi in range(n_blocks):
        rows = slice(qi * MOBA_BLOCK, (qi + 1) * MOBA_BLOCK)
        n_keys = (qi + 1) * MOBA_BLOCK
        qb = q_ref[rows, :].astype(BF16)
        s = _nt_dot(qb, kb[:n_keys]) * ATTN_SCALE
        parts = []
        if qi:
            sel = _moba_select(gate_all[rows], qi)
            parts = [jnp.where(sel[:, j:j + 1] > 0.0, s[:, j * MOBA_BLOCK:(j + 1) * MOBA_BLOCK], -jnp.inf)
                     for j in range(qi)]
        parts.append(jnp.where(causal, s[:, qi * MOBA_BLOCK:], -jnp.inf))
        p, l = _softmax_parts(jnp.concatenate(parts, axis=1) if qi else parts[0])
        o = jnp.dot(p.astype(BF16), vb[:n_keys], preferred_element_type=F32) / l
        o_ref[rows, :] = o.astype(o_ref.dtype)


def _moba_prompt(q, k, v, batch, seq):
    blk = pl.BlockSpec((seq, HEAD_DIM), lambda b, h: (b, h))
    return pl.pallas_call(
        functools.partial(_moba_prompt_kernel, n_blocks=seq // MOBA_BLOCK),
        grid=(batch, MOBA_HEADS),
        in_specs=[blk, blk, blk],
        out_specs=blk,
        out_shape=jax.ShapeDtypeStruct((batch * seq, MOBA_HEADS * HEAD_DIM), BF16),
        compiler_params=_cparams(2),
        name="moba_prompt",
    )(q, k, v)


def _lambda_value(lq1_ref, lk1_ref, lq2_ref, lk2_ref, lam_init):
    a = jnp.sum(lq1_ref[...] * lk1_ref[...], axis=1, keepdims=True)
    b = jnp.sum(lq2_ref[...] * lk2_ref[...], axis=1, keepdims=True)
    return jnp.exp(a) - jnp.exp(b) + lam_init


def _subln(o, g, lam_init):
    o = o * lax.rsqrt(jnp.mean(jnp.square(o), axis=-1, keepdims=True) + SUBLN_EPS)
    return o * g * (1.0 - lam_init)


DIFF_TQ = 256


def _diff_prompt_kernel(q_ref, k_ref, v_ref, lq1_ref, lk1_ref, lq2_ref, lk2_ref, g_ref, o_ref,
                        *, n_blocks, lam_init):
    lam = _lambda_value(lq1_ref, lk1_ref, lq2_ref, lk2_ref, lam_init)
    kb = k_ref[...].astype(BF16)
    vb = v_ref[...].astype(BF16)
    g = g_ref[...]
    causal = _causal_mask(DIFF_TQ)
    for qi in range(n_blocks):
        rows = slice(qi * DIFF_TQ, (qi + 1) * DIFF_TQ)
        n_keys = (qi + 1) * DIFF_TQ
        probs = []
        for c in range(2):
            cs = slice(c * HEAD_DIM, (c + 1) * HEAD_DIM)
            s = _nt_dot(q_ref[rows, cs].astype(BF16), kb[:n_keys, cs]) * ATTN_SCALE
            diag = jnp.where(causal, s[:, qi * DIFF_TQ:], -jnp.inf)
            s = jnp.concatenate([s[:, :qi * DIFF_TQ], diag], axis=1) if qi else diag
            probs.append(_softmax_parts(s))
        (p1, l1), (p2, l2) = probs
        a = p1 * (1.0 / l1) - p2 * (lam / l2)
        o = jnp.dot(a.astype(BF16), vb[:n_keys], preferred_element_type=F32)
        o_ref[rows, :] = _subln(o, g, lam_init).astype(o_ref.dtype)


def _diff_prompt(q, k, v, lams, g, batch, seq, lam_init):
    blk = pl.BlockSpec((seq, DIFF_V_DIM), lambda b, h: (b, h))
    vec = pl.BlockSpec((1, HEAD_DIM), lambda b, h: (0, 0))
    return pl.pallas_call(
        functools.partial(_diff_prompt_kernel, n_blocks=seq // DIFF_TQ, lam_init=lam_init),
        grid=(batch, DIFF_HEADS),
        in_specs=[blk, blk, blk, vec, vec, vec, vec, pl.BlockSpec((1, DIFF_V_DIM), lambda b, h: (0, 0))],
        out_specs=blk,
        out_shape=jax.ShapeDtypeStruct((batch * seq, DIFF_HEADS * DIFF_V_DIM), BF16),
        compiler_params=_cparams(2),
        name="diff_prompt",
    )(q, k, v, *lams, g)


def _layer_norm_rows(z, g, b):
    mu = jnp.mean(z, axis=-1, keepdims=True)
    zc = z - mu
    var = jnp.mean(zc * zc, axis=-1, keepdims=True)
    return zc * lax.rsqrt(var + LN_EPS) * g + b


def _route(logits):
    lane = lax.broadcasted_iota(jnp.int32, logits.shape, 1)
    lane_f = lane.astype(F32)
    big = float(logits.shape[1])
    lg = jnp.where(lane < N_GROUPS, logits, -jnp.inf)
    mg = jnp.max(lg, axis=1, keepdims=True)
    sg = jnp.sum(jnp.exp(lg - mg), axis=1, keepdims=True)
    g_val = 1.0 / sg
    g_idx = jnp.min(jnp.where(lg == mg, lane_f, big), axis=1, keepdims=True)
    lo = N_GROUPS + EXPERTS_PER_GROUP * g_idx
    in_grp = jnp.where(lane_f >= lo, jnp.where(lane_f < lo + EXPERTS_PER_GROUP, 1.0, 0.0), 0.0)
    le = jnp.where(in_grp > 0.0, logits, -jnp.inf)
    me = jnp.max(le, axis=1, keepdims=True)
    se = jnp.sum(jnp.exp(le - me), axis=1, keepdims=True)
    e1 = jnp.min(jnp.where(le == me, lane_f, big), axis=1, keepdims=True)
    le2 = jnp.where(lane_f == e1, -jnp.inf, le)
    m2 = jnp.max(le2, axis=1, keepdims=True)
    e2 = jnp.min(jnp.where(le2 == m2, lane_f, big), axis=1, keepdims=True)
    p1 = 1.0 / se
    p2 = jnp.exp(m2 - me) / se
    w1 = g_val * p1 / (p1 + p2)
    w2 = g_val * p2 / (p1 + p2)
    out = jnp.where(lane == 0, e1 - N_GROUPS,
                    jnp.where(lane == 1, e2 - N_GROUPS,
                              jnp.where(lane == 2, w1, jnp.where(lane == 3, w2, 0.0))))
    return out


def _ln_route_kernel(x_ref, a_ref, g_ref, b_ref, wr_ref, br_ref, y_ref, r_ref):
    y = _layer_norm_rows(DEEPNORM_ALPHA * x_ref[...] + a_ref[...], g_ref[...], b_ref[...])
    y_ref[...] = y
    logits = jnp.dot(y, wr_ref[...], precision=HIGHEST, preferred_element_type=F32) + br_ref[...]
    r_ref[...] = _route(logits)


def _ln_route(x, a, g, b, w_router, b_router, tm):
    m, d = x.shape
    row = pl.BlockSpec((tm, d), lambda i: (i, 0))
    vec = pl.BlockSpec((1, d), lambda i: (0, 0))
    return pl.pallas_call(
        _ln_route_kernel,
        grid=(m // tm,),
        in_specs=[row, row, vec, vec,
                  pl.BlockSpec((d, 128), lambda i: (0, 0)),
                  pl.BlockSpec((1, 128), lambda i: (0, 0))],
        out_specs=[row, pl.BlockSpec((tm, 128), lambda i: (i, 0))],
        out_shape=[jax.ShapeDtypeStruct((m, d), F32), jax.ShapeDtypeStruct((m, 128), F32)],
        compiler_params=_cparams(1),
        name="ln_route",
    )(x, a, g, b, w_router, b_router)


def _gather_rows_kernel(tok_ref, nt_ref, src_ref, o_ref, buf, sem):
    i = pl.program_id(0)
    n_tiles = nt_ref[0]

    def copy(tile, slot, r):
        return pltpu.make_async_copy(src_ref.at[pl.ds(tok_ref[tile * ROW_TILE + r], 1)],
                                     buf.at[slot, pl.ds(r, 1)], sem.at[slot])

    def issue(tile, slot):
        def body(r, c):
            copy(tile, slot, r).start()
            return c
        lax.fori_loop(0, ROW_TILE, body, 0, unroll=8)

    def drain(tile, slot):
        def body(r, c):
            copy(tile, slot, r).wait()
            return c
        lax.fori_loop(0, ROW_TILE, body, 0, unroll=8)

    @pl.when(jnp.logical_and(i == 0, n_tiles > 0))
    def _():
        issue(0, 0)

    @pl.when(i + 1 < n_tiles)
    def _():
        issue(i + 1, (i + 1) % 2)

    @pl.when(i < n_tiles)
    def _():
        drain(i, i % 2)
        o_ref[...] = buf[i % 2].astype(o_ref.dtype)

    @pl.when(i >= n_tiles)
    def _():
        o_ref[...] = jnp.zeros(o_ref.shape, o_ref.dtype)


def _gather_rows(src, row_token, n_tiles, r_max):
    d = src.shape[1]
    return pl.pallas_call(
        _gather_rows_kernel,
        grid_spec=pltpu.PrefetchScalarGridSpec(
            num_scalar_prefetch=2,
            grid=(r_max // ROW_TILE,),
            in_specs=[pl.BlockSpec(memory_space=pl.ANY)],
            out_specs=pl.BlockSpec((ROW_TILE, d), lambda i, tok, nt: (i, 0)),
            scratch_shapes=[pltpu.VMEM((2, ROW_TILE, d), src.dtype), pltpu.SemaphoreType.DMA((2,))],
        ),
        out_shape=jax.ShapeDtypeStruct((r_max, d), BF16),
        compiler_params=_cparams(1),
        name="moe_gather",
    )(row_token, n_tiles, src)


def _expert_changed(te_ref, i):
    prev = te_ref[jnp.maximum(i - 1, 0)]
    return jnp.logical_or(i == 0, te_ref[i] != prev)


def _moe_up_kernel(te_ref, nt_ref, x_ref, wg_ref, wu_ref, gate_ref, h_ref, wgb_ref, wub_ref):
    i = pl.program_id(1)

    @pl.when(_expert_changed(te_ref, i))
    def _():
        wgb_ref[...] = wg_ref[...].astype(BF16)
        wub_ref[...] = wu_ref[...].astype(BF16)

    @pl.when(i < nt_ref[0])
    def _():
        xb = x_ref[...]
        a = jnp.dot(xb, wgb_ref[...], preferred_element_type=F32)
        u = jnp.dot(xb, wub_ref[...], preferred_element_type=F32)
        h_ref[...] = (a * jax.nn.sigmoid(a) * u * gate_ref[...]).astype(h_ref.dtype)

    @pl.when(i >= nt_ref[0])
    def _():
        h_ref[...] = jnp.zeros(h_ref.shape, h_ref.dtype)


def _moe_up(x_sorted, w_gate, w_up, row_gate, tile_expert, n_tiles, layer, tf):
    r_max, d = x_sorted.shape
    t_max = r_max // ROW_TILE
    wspec = pl.BlockSpec((None, None, d, tf), lambda f, i, te, nt: (layer, te[i], 0, f))
    return pl.pallas_call(
        _moe_up_kernel,
        grid_spec=pltpu.PrefetchScalarGridSpec(
            num_scalar_prefetch=2,
            grid=(EXPERT_FF // tf, t_max),
            in_specs=[
                pl.BlockSpec((ROW_TILE, d), lambda f, i, te, nt: (i, 0)),
                wspec, wspec,
                pl.BlockSpec((ROW_TILE, 1), lambda f, i, te, nt: (i, 0)),
            ],
            out_specs=pl.BlockSpec((ROW_TILE, tf), lambda f, i, te, nt: (i, f)),
            scratch_shapes=[pltpu.VMEM((d, tf), BF16), pltpu.VMEM((d, tf), BF16)],
        ),
        out_shape=jax.ShapeDtypeStruct((r_max, EXPERT_FF), BF16),
        compiler_params=_cparams(2),
        name="moe_up",
    )(tile_expert, n_tiles, x_sorted, w_gate, w_up, row_gate)


def _moe_down_kernel(te_ref, nt_ref, h_ref, wd_ref, y_ref, wdb_ref):
    i = pl.program_id(1)

    @pl.when(_expert_changed(te_ref, i))
    def _():
        wdb_ref[...] = wd_ref[...].astype(BF16)

    @pl.when(i < nt_ref[0])
    def _():
        y_ref[...] = jnp.dot(h_ref[...], wdb_ref[...], preferred_element_type=F32)

    @pl.when(i >= nt_ref[0])
    def _():
        y_ref[...] = jnp.zeros(y_ref.shape, y_ref.dtype)


def _moe_down(h, w_down, tile_expert, n_tiles, layer, tn):
    r_max, ff = h.shape
    d = w_down.shape[-1]
    t_max = r_max // ROW_TILE
    return pl.pallas_call(
        _moe_down_kernel,
        grid_spec=pltpu.PrefetchScalarGridSpec(
            num_scalar_prefetch=2,
            grid=(d // tn, t_max),
            in_specs=[
                pl.BlockSpec((ROW_TILE, ff), lambda n, i, te, nt: (i, 0)),
                pl.BlockSpec((None, None, ff, tn), lambda n, i, te, nt: (layer, te[i], 0, n)),
            ],
            out_specs=pl.BlockSpec((ROW_TILE, tn), lambda n, i, te, nt: (i, n)),
            scratch_shapes=[pltpu.VMEM((ff, tn), BF16)],
        ),
        out_shape=jax.ShapeDtypeStruct((r_max, d), F32),
        compiler_params=_cparams(2),
        name="moe_down",
    )(tile_expert, n_tiles, h, w_down)


COMBINE_ROWS = 128


def _combine_ln_kernel(pos_ref, x_ref, g_ref, b_ref, ys_ref, y_ref, yb_ref, buf_ref, sem):
    i = pl.program_id(0)

    def copy(tile, slot, r, k):
        return pltpu.make_async_copy(ys_ref.at[pl.ds(pos_ref[2 * (tile * COMBINE_ROWS + r) + k], 1)],
                                     buf_ref.at[slot, k, pl.ds(r, 1)], sem.at[slot])

    def issue(tile, slot):
        def body(r, c):
            copy(tile, slot, r, 0).start()
            copy(tile, slot, r, 1).start()
            return c
        lax.fori_loop(0, COMBINE_ROWS, body, 0, unroll=4)

    def drain(tile, slot):
        def body(r, c):
            copy(tile, slot, r, 0).wait()
            copy(tile, slot, r, 1).wait()
            return c
        lax.fori_loop(0, COMBINE_ROWS, body, 0, unroll=4)

    @pl.when(i == 0)
    def _():
        issue(0, 0)

    @pl.when(i + 1 < pl.num_programs(0))
    def _():
        issue(i + 1, (i + 1) % 2)

    slot = i % 2
    drain(i, slot)
    f = buf_ref[slot, 0] + buf_ref[slot, 1]
    y = _layer_norm_rows(DEEPNORM_ALPHA * x_ref[...] + f, g_ref[...], b_ref[...])
    y_ref[...] = y
    yb_ref[...] = y.astype(BF16)


def _combine_ln(x, y_sorted, pos, g, b):
    m, d = x.shape
    row = pl.BlockSpec((COMBINE_ROWS, d), lambda i, pos: (i, 0))
    vec = pl.BlockSpec((1, d), lambda i, pos: (0, 0))
    return pl.pallas_call(
        _combine_ln_kernel,
        grid_spec=pltpu.PrefetchScalarGridSpec(
            num_scalar_prefetch=1,
            grid=(m // COMBINE_ROWS,),
            in_specs=[row, vec, vec, pl.BlockSpec(memory_space=pl.ANY)],
            out_specs=[row, row],
            scratch_shapes=[pltpu.VMEM((2, 2, COMBINE_ROWS, d), F32), pltpu.SemaphoreType.DMA((2,))],
        ),
        out_shape=[jax.ShapeDtypeStruct((m, d), F32), jax.ShapeDtypeStruct((m, d), BF16)],
        compiler_params=_cparams(1),
        name="moe_combine_ln",
    )(pos, x, g, b, y_sorted)


def _dispatch(route, n_valid, r_max):
    e = route[:n_valid, 0:2].astype(jnp.int32).reshape(-1)
    w = route[:n_valid, 2:4].reshape(-1)
    onehot = (e[:, None] == jnp.arange(N_EXPERTS, dtype=jnp.int32)[None, :]).astype(jnp.int32)
    csum = jnp.cumsum(onehot, axis=0)
    counts = csum[-1]
    rank = jnp.sum(csum * onehot, axis=1) - 1
    tiles_per = (counts + ROW_TILE - 1) // ROW_TILE
    tile_end = jnp.cumsum(tiles_per)
    tile_start = tile_end - tiles_per
    pos = tile_start[e] * ROW_TILE + rank
    n_tiles = tile_end[-1]
    t_max = r_max // ROW_TILE
    tile_ids = jnp.arange(t_max, dtype=jnp.int32)
    tile_expert = jnp.sum((tile_ids[:, None] >= tile_end[None, :]).astype(jnp.int32), axis=1)
    last_expert = jnp.sum((n_tiles - 1 >= tile_end).astype(jnp.int32))
    tile_expert = jnp.where(tile_ids < n_tiles, tile_expert, last_expert).astype(jnp.int32)
    row_assign = jnp.zeros((r_max,), jnp.int32).at[pos].set(jnp.arange(2 * n_valid, dtype=jnp.int32))
    row_token = row_assign // 2
    row_gate = w[row_assign]
    return pos.astype(jnp.int32), row_token, row_gate[:, None], tile_expert, n_tiles.astype(jnp.int32)


KMEAN_PAGES = 8
PAGES_PER_BLOCK = MOBA_BLOCK // PAGE_SIZE


def _kmean_kernel(pt_ref, *refs):
    pages, o_ref = refs[:-1], refs[-1]
    for blk in range(KMEAN_PAGES // PAGES_PER_BLOCK):
        tot = jnp.sum(pages[2 * blk][...], axis=0) + jnp.sum(pages[2 * blk + 1][...], axis=0)
        o_ref[blk] = tot * (1.0 / MOBA_BLOCK)


def _kmean(cache_k, page_table, layer):
    db, n_pages = page_table.shape
    steps = n_pages // KMEAN_PAGES
    bps = KMEAN_PAGES // PAGES_PER_BLOCK

    def page_spec(kk):
        return pl.BlockSpec((None, None, PAGE_SIZE, MOBA_HEADS, HEAD_DIM),
                            lambda b, s, pt: (layer, pt[b, s * KMEAN_PAGES + kk], 0, 0, 0))

    return pl.pallas_call(
        _kmean_kernel,
        grid_spec=pltpu.PrefetchScalarGridSpec(
            num_scalar_prefetch=1,
            grid=(db, steps),
            in_specs=[page_spec(kk) for kk in range(KMEAN_PAGES)],
            out_specs=pl.BlockSpec((None, bps, MOBA_HEADS, HEAD_DIM), lambda b, s, pt: (b, s, 0, 0)),
        ),
        out_shape=jax.ShapeDtypeStruct((db, n_pages // PAGES_PER_BLOCK, MOBA_HEADS, HEAD_DIM), F32),
        compiler_params=_cparams(2),
        name="moba_sample_kmean",
    )(page_table, *([cache_k] * KMEAN_PAGES))


def _topk_kernel(km_ref, q_ref, idx_ref, keep_ref):
    gate = jnp.sum(km_ref[...] * q_ref[...][None], axis=-1, keepdims=True)
    nb = gate.shape[0]
    bid = lax.broadcasted_iota(jnp.int32, gate.shape, 0)
    for r in range(MOBA_TOPK):
        m = jnp.max(gate, axis=0, keepdims=True)
        first = jnp.min(jnp.where(gate == m, bid, nb), axis=0, keepdims=True)
        idx_ref[r] = first[0]
        keep_ref[r] = jnp.where(jnp.abs(m[0]) < jnp.inf, 1, 0)
        gate = jnp.where(bid == first, -jnp.inf, gate)


def _sample_topk(k_mean, q):
    db, nb = k_mean.shape[:2]
    out = jax.ShapeDtypeStruct((db, MOBA_TOPK, MOBA_HEADS, 1), jnp.int32)
    ospec = pl.BlockSpec((None, MOBA_TOPK, MOBA_HEADS, 1), lambda b: (b, 0, 0, 0))
    return pl.pallas_call(
        _topk_kernel,
        grid=(db,),
        in_specs=[pl.BlockSpec((None, nb, MOBA_HEADS, HEAD_DIM), lambda b: (b, 0, 0, 0)),
                  pl.BlockSpec((None, MOBA_HEADS, HEAD_DIM), lambda b: (b, 0, 0))],
        out_specs=[ospec, ospec],
        out_shape=[out, out],
        compiler_params=_cparams(1),
        name="moba_sample_topk",
    )(k_mean, q)


SEL_ROWS = MOBA_TOPK * MOBA_BLOCK


def _moba_sample_kernel(page_ref, keep_ref, q_ref, kn_ref, vn_ref, ck_ref, cv_ref, o_ref,
                        kbuf, vbuf, sem, *, layer):
    b = pl.program_id(0)
    n_sel_pages = MOBA_TOPK * PAGES_PER_BLOCK

    def copies(h, j):
        page = page_ref[(b * MOBA_HEADS + h) * n_sel_pages + j]
        dst = pl.ds(j * PAGE_SIZE, PAGE_SIZE)
        return (pltpu.make_async_copy(ck_ref.at[layer, page, :, h, :], kbuf.at[h, dst, :], sem.at[0]),
                pltpu.make_async_copy(cv_ref.at[layer, page, :, h, :], vbuf.at[h, dst, :], sem.at[1]))

    def start(t, c):
        ck, cv = copies(t // n_sel_pages, t % n_sel_pages)
        ck.start()
        cv.start()
        return c

    def wait(t, c):
        ck, cv = copies(t // n_sel_pages, t % n_sel_pages)
        ck.wait()
        cv.wait()
        return c

    lax.fori_loop(0, MOBA_HEADS * n_sel_pages, start, 0)
    lax.fori_loop(0, MOBA_HEADS * n_sel_pages, wait, 0)

    q = q_ref[...]
    s = jnp.sum(kbuf[...] * q[:, None, :], axis=-1, keepdims=True) * ATTN_SCALE
    rid = lax.broadcasted_iota(jnp.int32, s.shape, 1)
    keep = keep_ref[...]
    kept = jnp.zeros(s.shape, jnp.int32)
    for r in range(MOBA_TOPK):
        kept = jnp.where(rid // MOBA_BLOCK == r, keep[r][:, None, :], kept)
    s = jnp.where(kept > 0, s, -jnp.inf)
    s_new = jnp.sum(kn_ref[...] * q, axis=-1, keepdims=True)[:, None, :] * ATTN_SCALE
    m = jnp.maximum(jnp.max(s, axis=1, keepdims=True), s_new)
    p = jnp.exp(s - m)
    p_new = jnp.exp(s_new - m)
    denom = jnp.sum(p, axis=1, keepdims=True) + p_new
    num = jnp.sum(p * vbuf[...], axis=1, keepdims=True) + p_new * vn_ref[...][:, None, :]
    o_ref[...] = (num / denom)[:, 0, :]


def _moba_sample(q, k_new, v_new, cache_k, cache_v, sel_pages, keep, layer):
    db = q.shape[0]
    hd = pl.BlockSpec((None, MOBA_HEADS, HEAD_DIM), lambda b, pg: (b, 0, 0))
    return pl.pallas_call(
        functools.partial(_moba_sample_kernel, layer=layer),
        grid_spec=pltpu.PrefetchScalarGridSpec(
            num_scalar_prefetch=1,
            grid=(db,),
            in_specs=[pl.BlockSpec((None, MOBA_TOPK, MOBA_HEADS, 1), lambda b, pg: (b, 0, 0, 0)),
                      hd, hd, hd,
                      pl.BlockSpec(memory_space=pl.ANY), pl.BlockSpec(memory_space=pl.ANY)],
            out_specs=hd,
            scratch_shapes=[pltpu.VMEM((MOBA_HEADS, SEL_ROWS, HEAD_DIM), F32),
                            pltpu.VMEM((MOBA_HEADS, SEL_ROWS, HEAD_DIM), F32),
                            pltpu.SemaphoreType.DMA((2,))],
        ),
        out_shape=jax.ShapeDtypeStruct((db, MOBA_HEADS, HEAD_DIM), F32),
        compiler_params=_cparams(1),
        name="moba_sample_attend",
    )(sel_pages, keep, q, k_new, v_new, cache_k, cache_v)


DIFF_PAGES = 4


def _diff_sample_kernel(pt_ref, *refs, lam_init):
    kp = refs[:DIFF_PAGES]
    vp = refs[DIFF_PAGES:2 * DIFF_PAGES]
    (q_ref, kn_ref, vn_ref, lq1_ref, lk1_ref, lq2_ref, lk2_ref, g_ref, o_ref,
     m_ref, l_ref, acc_ref) = refs[2 * DIFF_PAGES:]
    step = pl.program_id(1)

    @pl.when(step == 0)
    def _():
        for c in range(2):
            s_new = jnp.sum(kn_ref[c] * q_ref[c], axis=-1, keepdims=True) * ATTN_SCALE
            m_ref[c] = s_new
            l_ref[c] = jnp.ones(s_new.shape, F32)
            acc_ref[c] = vn_ref[...]

    for c in range(2):
        qc = q_ref[c]
        s = jnp.concatenate(
            [jnp.sum(kp[p][:, pl.ds(c, DIFF_HEADS, stride=2), :] * qc[None], axis=-1, keepdims=True)
             for p in range(DIFF_PAGES)], axis=0) * ATTN_SCALE
        m_old = m_ref[c]
        m_new = jnp.maximum(m_old, jnp.max(s, axis=0))
        alpha = jnp.exp(m_old - m_new)
        pr = jnp.exp(s - m_new[None])
        l_ref[c] = alpha * l_ref[c] + jnp.sum(pr, axis=0)
        pv = jnp.zeros(acc_ref.shape[1:], F32)
        for p in range(DIFF_PAGES):
            pv = pv + jnp.sum(pr[p * PAGE_SIZE:(p + 1) * PAGE_SIZE] * vp[p][...], axis=0)
        acc_ref[c] = alpha * acc_ref[c] + pv
        m_ref[c] = m_new

    @pl.when(step == pl.num_programs(1) - 1)
    def _():
        lam = _lambda_value(lq1_ref, lk1_ref, lq2_ref, lk2_ref, lam_init)
        o = acc_ref[0] / l_ref[0] - lam * (acc_ref[1] / l_ref[1])
        o_ref[...] = _subln(o, g_ref[...], lam_init)


def _diff_sample(q, k_new, v_new, cache_k, cache_v, page_table, lams, g, layer, lam_init):
    db, n_pages = page_table.shape
    steps = n_pages // DIFF_PAGES

    def kspec(kk):
        return pl.BlockSpec((None, None, PAGE_SIZE, 2 * DIFF_HEADS, HEAD_DIM),
                            lambda b, s, pt: (layer, pt[b, s * DIFF_PAGES + kk], 0, 0, 0))

    def vspec(kk):
        return pl.BlockSpec((None, None, PAGE_SIZE, DIFF_HEADS, DIFF_V_DIM),
                            lambda b, s, pt: (layer, pt[b, s * DIFF_PAGES + kk], 0, 0, 0))

    qk = pl.BlockSpec((None, 2, DIFF_HEADS, HEAD_DIM), lambda b, s, pt: (b, 0, 0, 0))
    vo = pl.BlockSpec((None, DIFF_HEADS, DIFF_V_DIM), lambda b, s, pt: (b, 0, 0))
    vec = pl.BlockSpec((1, HEAD_DIM), lambda b, s, pt: (0, 0))
    return pl.pallas_call(
        functools.partial(_diff_sample_kernel, lam_init=lam_init),
        grid_spec=pltpu.PrefetchScalarGridSpec(
            num_scalar_prefetch=1,
            grid=(db, steps),
            in_specs=([kspec(kk) for kk in range(DIFF_PAGES)] + [vspec(kk) for kk in range(DIFF_PAGES)]
                      + [qk, qk, vo, vec, vec, vec, vec,
                         pl.BlockSpec((1, DIFF_V_DIM), lambda b, s, pt: (0, 0))]),
            out_specs=vo,
            scratch_shapes=[pltpu.VMEM((2, DIFF_HEADS, 1), F32), pltpu.VMEM((2, DIFF_HEADS, 1), F32),
                            pltpu.VMEM((2, DIFF_HEADS, DIFF_V_DIM), F32)],
        ),
        out_shape=jax.ShapeDtypeStruct((db, DIFF_HEADS, DIFF_V_DIM), F32),
        compiler_params=_cparams(2),
        name="diff_sample",
    )(page_table, *([cache_k] * DIFF_PAGES), *([cache_v] * DIFF_PAGES), q, k_new, v_new, *lams, g)


def _rope_tables(positions):
    half = HEAD_DIM // 2
    inv_freq = ROPE_THETA ** (-jnp.arange(half, dtype=F32) / half)
    ang = positions.astype(F32)[:, None] * inv_freq[None, :]
    cos, sin = jnp.cos(ang), jnp.sin(ang)
    return jnp.concatenate([cos, cos], axis=-1), jnp.concatenate([-sin, sin], axis=-1)


def kernel(x_prompt, x_sample, cache_moba_k, cache_moba_v, cache_diff_k, cache_diff_v, page_table, w_in, w_out, lambda_q1, lambda_k1, lambda_q2, lambda_k2, subln_g, ln1_g, ln1_b, w_router_group, b_router_group, w_router_expert, b_router_expert, w_exp_gate, w_exp_up, w_exp_down, ln2_g, ln2_b):
    batch, seq, d = x_prompt.shape
    db = x_sample.shape[0]
    depth = w_in.shape[0]
    n_pages = page_table.shape[1]
    past = n_pages * PAGE_SIZE
    m_prompt = batch * seq
    n_valid = m_prompt + db
    m_pad = -(-n_valid // (3 * ROW_TILE)) * (3 * ROW_TILE)
    tm_big = 3 * ROW_TILE
    r_max = ((2 * n_valid + N_EXPERTS * (ROW_TILE - 1)) // ROW_TILE + 1) * ROW_TILE

    x = jnp.concatenate([x_prompt.reshape(m_prompt, d), x_sample.reshape(db, d),
                         jnp.zeros((m_pad - n_valid, d), F32)], axis=0)
    xb = x.astype(BF16)
    positions = jnp.concatenate([jnp.tile(jnp.arange(seq, dtype=jnp.int32), batch),
                                 jnp.full((db,), past, jnp.int32),
                                 jnp.zeros((m_pad - n_valid,), jnp.int32)])
    cos, sin = _rope_tables(positions)

    cdk = cache_diff_k.reshape(cache_diff_k.shape[:3] + (2 * DIFF_HEADS, HEAD_DIM))

    outs = [[] for _ in range(8)]
    for l in range(depth):
        lam_init = 0.8 - 0.6 * math.exp(-0.3 * l)
        lams = (lambda_q1[l][None], lambda_k1[l][None], lambda_q2[l][None], lambda_k2[l][None])
        g_sub = subln_g[l][None]

        proj = functools.partial(_project, xb, w_in, l, cos=cos, sin=sin, tm=tm_big, tn=512)
        qm = proj(0 * GROUP_WIDTH, GROUP_WIDTH, rope=True)
        km = proj(1 * GROUP_WIDTH, GROUP_WIDTH, rope=True)
        vm = proj(2 * GROUP_WIDTH, GROUP_WIDTH, rope=False)
        qd = proj(3 * GROUP_WIDTH, GROUP_WIDTH, rope=True)
        kd = proj(4 * GROUP_WIDTH, GROUP_WIDTH, rope=True)
        vd = proj(5 * GROUP_WIDTH, GROUP_WIDTH, rope=False)

        om_p = _moba_prompt(qm, km, vm, batch, seq)
        od_p = _diff_prompt(qd, kd, vd, lams, g_sub, batch, seq, lam_init)

        sl = slice(m_prompt, n_valid)
        qm_s = qm[sl].reshape(db, MOBA_HEADS, HEAD_DIM)
        km_s = km[sl].reshape(db, MOBA_HEADS, HEAD_DIM)
        vm_s = vm[sl].reshape(db, MOBA_HEADS, HEAD_DIM)
        k_mean = _kmean(cache_moba_k, page_table, l)
        sel_idx, keep = _sample_topk(k_mean, qm_s)
        blk = jnp.transpose(sel_idx[..., 0], (0, 2, 1))
        logical = blk[..., None] * PAGES_PER_BLOCK + jnp.arange(PAGES_PER_BLOCK, dtype=jnp.int32)
        phys = jnp.take_along_axis(page_table[:, None, :], logical.reshape(db, MOBA_HEADS, -1), axis=2)
        om_s = _moba_sample(qm_s, km_s, vm_s, cache_moba_k, cache_moba_v,
                            phys.reshape(-1).astype(jnp.int32), keep, l)

        qd_s = jnp.transpose(qd[sl].reshape(db, DIFF_HEADS, 2, HEAD_DIM), (0, 2, 1, 3))
        kd_s = jnp.transpose(kd[sl].reshape(db, DIFF_HEADS, 2, HEAD_DIM), (0, 2, 1, 3))
        vd_s = vd[sl].reshape(db, DIFF_HEADS, DIFF_V_DIM)
        od_s = _diff_sample(qd_s, kd_s, vd_s, cdk, cache_diff_v, page_table, lams, g_sub, l, lam_init)

        cat = jnp.concatenate([
            jnp.concatenate([om_p, od_p], axis=1),
            jnp.concatenate([om_s.reshape(db, -1), od_s.reshape(db, -1)], axis=1).astype(BF16),
            jnp.zeros((m_pad - n_valid, d), BF16)], axis=0)
        a = _project(cat, w_out, l, 0, d, cos, sin, rope=False, tm=tm_big, tn=512)

        w_router = jnp.concatenate([w_router_group[l], w_router_expert[l],
                                    jnp.zeros((d, 128 - N_GROUPS - N_EXPERTS), F32)], axis=1)
        b_router = jnp.concatenate([b_router_group[l], b_router_expert[l],
                                    jnp.zeros((128 - N_GROUPS - N_EXPERTS,), F32)])[None]
        h1, route = _ln_route(x, a, ln1_g[l][None], ln1_b[l][None], w_router, b_router, ROW_TILE)

        pos, row_token, row_gate, tile_expert, n_tiles = _dispatch(route, n_valid, r_max)
        x_sorted = _gather_rows(h1, row_token, n_tiles[None], r_max)
        hid = _moe_up(x_sorted, w_exp_gate, w_exp_up, row_gate, tile_expert, n_tiles[None], l, 512)
        y_sorted = _moe_down(hid, w_exp_down, tile_expert, n_tiles[None], l, 2048)
        pos_pad = jnp.concatenate([pos, jnp.zeros((2 * (m_pad - n_valid),), jnp.int32)])
        x, xb = _combine_ln(h1, y_sorted, pos_pad, ln2_g[l][None], ln2_b[l][None])

        outs[0].append(km[:m_prompt].reshape(batch, seq, MOBA_HEADS, HEAD_DIM))
        outs[1].append(vm[:m_prompt].reshape(batch, seq, MOBA_HEADS, HEAD_DIM))
        outs[2].append(kd[:m_prompt].reshape(batch, seq, DIFF_HEADS, 2, HEAD_DIM))
        outs[3].append(vd[:m_prompt].reshape(batch, seq, DIFF_HEADS, DIFF_V_DIM))
        outs[4].append(km[sl].reshape(db, 1, MOBA_HEADS, HEAD_DIM))
        outs[5].append(vm[sl].reshape(db, 1, MOBA_HEADS, HEAD_DIM))
        outs[6].append(kd[sl].reshape(db, 1, DIFF_HEADS, 2, HEAD_DIM))
        outs[7].append(vd[sl].reshape(db, 1, DIFF_HEADS, DIFF_V_DIM))

    y_prompt = x[:m_prompt].reshape(batch, seq, d)
    y_sample = x[m_prompt:n_valid].reshape(db, 1, d)
    return (y_prompt, y_sample) + tuple(jnp.stack(o) for o in outs)
```

```python
import functools
import math

import jax
import jax.numpy as jnp
from jax import lax
from jax.experimental import pallas as pl
from jax.experimental.pallas import tpu as pltpu

F32 = jnp.float32
BF16 = jnp.bfloat16
HIGHEST = lax.Precision.HIGHEST

D_MODEL = 4096
HEAD_DIM = 128
MOBA_HEADS = 16
DIFF_HEADS = 8
DIFF_V_DIM = 256
GROUP_WIDTH = 2048
MOBA_BLOCK = 256
MOBA_TOPK = 3
PAGE_SIZE = 128
ROPE_THETA = 10000.0
N_GROUPS = 4
EXPERTS_PER_GROUP = 8
N_EXPERTS = 32
EXPERT_FF = 1024
DEPTH = 2
DEEPNORM_ALPHA = (2 * DEPTH) ** 0.25
LN_EPS = 1e-5
SUBLN_EPS = 1e-5
ATTN_SCALE = HEAD_DIM ** -0.5

ROW_TILE = 256
VMEM_LIMIT = 56 * 1024 * 1024


def _cparams(n_axes):
    return pltpu.CompilerParams(dimension_semantics=("arbitrary",) * n_axes,
                                vmem_limit_bytes=VMEM_LIMIT)


def _proj_kernel(x_ref, w_ref, cos_ref, sin_ref, o_ref, *rest, rope, head_width, rows_last):
    j, i = pl.program_id(0), pl.program_id(1)
    wb_ref = rest[0] if head_width is None else rest[1]

    @pl.when(i == 0)
    def _():
        wb_ref[...] = w_ref[...].astype(BF16)

    acc = jnp.dot(x_ref[...], wb_ref[...], preferred_element_type=F32)
    if rope:
        cos = cos_ref[...]
        sin = sin_ref[...]
        for h in range(acc.shape[1] // HEAD_DIM):
            z = acc[:, h * HEAD_DIM:(h + 1) * HEAD_DIM]
            o_ref[:, h * HEAD_DIM:(h + 1) * HEAD_DIM] = (
                z * cos + pltpu.roll(z, HEAD_DIM // 2, 1) * sin)
    else:
        o_ref[...] = acc

    if head_width is not None:
        heads_ref, _, stage_ref, sem = rest
        tm, tn = o_ref.shape
        heads_per_block = tn // head_width
        n_i = pl.num_programs(1)
        step = j * n_i + i
        last = pl.num_programs(0) * n_i - 1
        slot = step % 2

        def copies(s, tile, n_rows):
            return [pltpu.make_async_copy(
                stage_ref.at[s, 0:n_rows, hh * head_width:(hh + 1) * head_width],
                heads_ref.at[pl.ds(tile * tm, n_rows), j * heads_per_block + hh, :],
                sem.at[s]) for hh in range(heads_per_block)]

        def for_tile(s, tile, action):
            @pl.when(tile == n_i - 1)
            def _():
                for c in copies(s, tile, rows_last):
                    action(c)

            @pl.when(tile != n_i - 1)
            def _():
                for c in copies(s, tile, tm):
                    action(c)

        @pl.when(step >= 2)
        def _():
            for_tile(slot, lax.rem(step - 2, n_i), lambda c: c.wait())

        stage_ref[slot] = o_ref[...]
        for_tile(slot, i, lambda c: c.start())

        @pl.when(step == last)
        def _():
            for_tile(slot, i, lambda c: c.wait())

        @pl.when(jnp.logical_and(step == last, step >= 1))
        def _():
            for_tile(1 - slot, lax.rem(step - 1, n_i), lambda c: c.wait())


def _project(x_bf16, w, layer, col0, n_cols, cos, sin, *, rope, tm, tn, head_width=None, head_rows=None):
    m, k = x_bf16.shape
    jb0 = col0 // tn
    out_specs = [pl.BlockSpec((tm, tn), lambda j, i: (i, j))]
    out_shape = [jax.ShapeDtypeStruct((m, n_cols), F32)]
    scratch = [pltpu.VMEM((k, tn), BF16)]
    rows_last = None
    if head_width is not None:
        rows_last = head_rows - (m // tm - 1) * tm
        assert 0 < rows_last <= tm and rows_last % 8 == 0, (head_rows, tm)
        out_specs.append(pl.BlockSpec(memory_space=pl.ANY))
        out_shape.append(jax.ShapeDtypeStruct((head_rows, n_cols // head_width, head_width), F32))
        scratch += [pltpu.VMEM((2, tm, tn), F32), pltpu.SemaphoreType.DMA((2,))]
    outs = pl.pallas_call(
        functools.partial(_proj_kernel, rope=rope, head_width=head_width, rows_last=rows_last),
        grid=(n_cols // tn, m // tm),
        in_specs=[
            pl.BlockSpec((tm, k), lambda j, i: (i, 0)),
            pl.BlockSpec((None, k, tn), lambda j, i: (layer, 0, jb0 + j)),
            pl.BlockSpec((tm, HEAD_DIM), lambda j, i: (i, 0)),
            pl.BlockSpec((tm, HEAD_DIM), lambda j, i: (i, 0)),
        ],
        out_specs=out_specs,
        out_shape=out_shape,
        scratch_shapes=scratch,
        compiler_params=_cparams(2),
        name=("proj_rope" if rope else "proj") + ("" if head_width is None else "_heads"),
    )(x_bf16, w, cos, sin)
    return outs[0] if head_width is None else outs


def _nt_dot(a, b, **kw):
    return lax.dot_general(a, b, (((1,), (1,)), ((), ())), preferred_element_type=F32, **kw)


def _causal_mask(n):
    r_id = lax.broadcasted_iota(jnp.int32, (n, n), 0)
    c_id = lax.broadcasted_iota(jnp.int32, (n, n), 1)
    return c_id <= r_id


def _softmax_parts(s):
    m = jnp.max(s, axis=1, keepdims=True)
    p = jnp.exp(s - m)
    return p, jnp.sum(p, axis=1, keepdims=True)


def _moba_select(gate, qi):
    blk = lax.broadcasted_iota(jnp.int32, gate.shape, 1)
    gate = jnp.where(blk < qi, gate, -jnp.inf)
    finite = jnp.where(jnp.abs(gate) < jnp.inf, 1.0, 0.0)
    if qi <= MOBA_TOPK:
        return finite
    rank = jnp.zeros(gate.shape, jnp.int32)
    for n2 in range(qi):
        col = gate[:, n2:n2 + 1]
        rank = rank + jnp.where(col > gate, 1, jnp.where(col == gate, jnp.where(n2 < blk, 1, 0), 0))
    return jnp.where(rank < MOBA_TOPK, finite, 0.0)


def _moba_prompt_kernel(q_ref, k_ref, v_ref, o_ref, *, n_blocks):
    k = k_ref[...]
    kb = k.astype(BF16)
    vb = v_ref[...].astype(BF16)
    k_mean = jnp.mean(k.reshape(n_blocks, MOBA_BLOCK, HEAD_DIM), axis=1)
    gate_all = _nt_dot(q_ref[...], k_mean, precision=HIGHEST)
    causal = _causal_mask(MOBA_BLOCK)
    for qi in range(n_blocks):
        rows = slice(qi * MOBA_BLOCK, (qi + 1) * MOBA_BLOCK)
        n_keys = (qi + 1) * MOBA_BLOCK
        qb = q_ref[rows, :].astype(BF16)
        s = _nt_dot(qb, kb[:n_keys]) * ATTN_SCALE
        parts = []
        if qi:
            sel = _moba_select(gate_all[rows], qi)
            parts = [jnp.where(sel[:, j:j + 1] > 0.0, s[:, j * MOBA_BLOCK:(j + 1) * MOBA_BLOCK], -jnp.inf)
                     for j in range(qi)]
        parts.append(jnp.where(causal, s[:, qi * MOBA_BLOCK:], -jnp.inf))
        p, l = _softmax_parts(jnp.concatenate(parts, axis=1) if qi else parts[0])
        o = jnp.dot(p.astype(BF16), vb[:n_keys], preferred_element_type=F32) / l
        o_ref[rows, :] = o.astype(o_ref.dtype)


def _moba_prompt(q, k, v, batch, seq):
    blk = pl.BlockSpec((seq, HEAD_DIM), lambda b, h: (b, h))
    return pl.pallas_call(
        functools.partial(_moba_prompt_kernel, n_blocks=seq // MOBA_BLOCK),
        grid=(batch, MOBA_HEADS),
        in_specs=[blk, blk, blk],
        out_specs=blk,
        out_shape=jax.ShapeDtypeStruct((batch * seq, MOBA_HEADS * HEAD_DIM), BF16),
        compiler_params=_cparams(2),
        name="moba_prompt",
    )(q, k, v)


def _lambda_value(lq1_ref, lk1_ref, lq2_ref, lk2_ref, lam_init):
    a = jnp.sum(lq1_ref[...] * lk1_ref[...], axis=1, keepdims=True)
    b = jnp.sum(lq2_ref[...] * lk2_ref[...], axis=1, keepdims=True)
    return jnp.exp(a) - jnp.exp(b) + lam_init


def _subln(o, g, lam_init):
    o = o * lax.rsqrt(jnp.mean(jnp.square(o), axis=-1, keepdims=True) + SUBLN_EPS)
    return o * g * (1.0 - lam_init)


DIFF_TQ = 256


def _diff_prompt_kernel(q_ref, k_ref, v_ref, lq1_ref, lk1_ref, lq2_ref, lk2_ref, g_ref, o_ref,
                        *, n_blocks, lam_init):
    lam = _lambda_value(lq1_ref, lk1_ref, lq2_ref, lk2_ref, lam_init)
    kb = k_ref[...].astype(BF16)
    vb = v_ref[...].astype(BF16)
    g = g_ref[...]
    causal = _causal_mask(DIFF_TQ)
    for qi in range(n_blocks):
        rows = slice(qi * DIFF_TQ, (qi + 1) * DIFF_TQ)
        n_keys = (qi + 1) * DIFF_TQ
        probs = []
        for c in range(2):
            cs = slice(c * HEAD_DIM, (c + 1) * HEAD_DIM)
            s = _nt_dot(q_ref[rows, cs].astype(BF16), kb[:n_keys, cs]) * ATTN_SCALE
            diag = jnp.where(causal, s[:, qi * DIFF_TQ:], -jnp.inf)
            s = jnp.concatenate([s[:, :qi * DIFF_TQ], diag], axis=1) if qi else diag
            probs.append(_softmax_parts(s))
        (p1, l1), (p2, l2) = probs
        a = p1 * (1.0 / l1) - p2 * (lam / l2)
        o = jnp.dot(a.astype(BF16), vb[:n_keys], preferred_element_type=F32)
        o_ref[rows, :] = _subln(o, g, lam_init).astype(o_ref.dtype)


def _diff_prompt(q, k, v, lams, g, batch, seq, lam_init):
    blk = pl.BlockSpec((seq, DIFF_V_DIM), lambda b, h: (b, h))
    vec = pl.BlockSpec((1, HEAD_DIM), lambda b, h: (0, 0))
    return pl.pallas_call(
        functools.partial(_diff_prompt_kernel, n_blocks=seq // DIFF_TQ, lam_init=lam_init),
        grid=(batch, DIFF_HEADS),
        in_specs=[blk, blk, blk, vec, vec, vec, vec, pl.BlockSpec((1, DIFF_V_DIM), lambda b, h: (0, 0))],
        out_specs=blk,
        out_shape=jax.ShapeDtypeStruct((batch * seq, DIFF_HEADS * DIFF_V_DIM), BF16),
        compiler_params=_cparams(2),
        name="diff_prompt",
    )(q, k, v, *lams, g)


def _layer_norm_rows(z, g, b):
    mu = jnp.mean(z, axis=-1, keepdims=True)
    zc = z - mu
    var = jnp.mean(zc * zc, axis=-1, keepdims=True)
    return zc * lax.rsqrt(var + LN_EPS) * g + b


def _route(logits):
    lane = lax.broadcasted_iota(jnp.int32, logits.shape, 1)
    lane_f = lane.astype(F32)
    big = float(logits.shape[1])
    lg = jnp.where(lane < N_GROUPS, logits, -jnp.inf)
    mg = jnp.max(lg, axis=1, keepdims=True)
    sg = jnp.sum(jnp.exp(lg - mg), axis=1, keepdims=True)
    g_val = 1.0 / sg
    g_idx = jnp.min(jnp.where(lg == mg, lane_f, big), axis=1, keepdims=True)
    lo = N_GROUPS + EXPERTS_PER_GROUP * g_idx
    in_grp = jnp.where(lane_f >= lo, jnp.where(lane_f < lo + EXPERTS_PER_GROUP, 1.0, 0.0), 0.0)
    le = jnp.where(in_grp > 0.0, logits, -jnp.inf)
    me = jnp.max(le, axis=1, keepdims=True)
    se = jnp.sum(jnp.exp(le - me), axis=1, keepdims=True)
    e1 = jnp.min(jnp.where(le == me, lane_f, big), axis=1, keepdims=True)
    le2 = jnp.where(lane_f == e1, -jnp.inf, le)
    m2 = jnp.max(le2, axis=1, keepdims=True)
    e2 = jnp.min(jnp.where(le2 == m2, lane_f, big), axis=1, keepdims=True)
    p1 = 1.0 / se
    p2 = jnp.exp(m2 - me) / se
    w1 = g_val * p1 / (p1 + p2)
    w2 = g_val * p2 / (p1 + p2)
    out = jnp.where(lane == 0, e1 - N_GROUPS,
                    jnp.where(lane == 1, e2 - N_GROUPS,
                              jnp.where(lane == 2, w1, jnp.where(lane == 3, w2, 0.0))))
    return out


def _pack_bf16_pairs(y):
    half = y.shape[1] // 2
    lo = lax.bitcast_convert_type(y[:, :half].astype(BF16).astype(F32), jnp.uint32)
    hi = lax.bitcast_convert_type(y[:, half:].astype(BF16).astype(F32), jnp.uint32)
    return lax.shift_right_logical(lo, jnp.uint32(16)) | (hi & jnp.uint32(0xFFFF0000))


def _unpack_bf16_pairs(u):
    lo = lax.bitcast_convert_type(lax.shift_left(u, jnp.uint32(16)), F32)
    hi = lax.bitcast_convert_type(u & jnp.uint32(0xFFFF0000), F32)
    return lo.astype(BF16), hi.astype(BF16)


def _ln_route_kernel(x_ref, a_ref, g_ref, b_ref, wr_ref, br_ref, y_ref, yp_ref, r_ref):
    y = _layer_norm_rows(DEEPNORM_ALPHA * x_ref[...] + a_ref[...], g_ref[...], b_ref[...])
    y_ref[...] = y
    yp_ref[...] = _pack_bf16_pairs(y)
    logits = jnp.dot(y, wr_ref[...], precision=HIGHEST, preferred_element_type=F32) + br_ref[...]
    r_ref[...] = _route(logits)


def _ln_route(x, a, g, b, w_router, b_router, tm):
    m, d = x.shape
    row = pl.BlockSpec((tm, d), lambda i: (i, 0))
    vec = pl.BlockSpec((1, d), lambda i: (0, 0))
    return pl.pallas_call(
        _ln_route_kernel,
        grid=(m // tm,),
        in_specs=[row, row, vec, vec,
                  pl.BlockSpec((d, 128), lambda i: (0, 0)),
                  pl.BlockSpec((1, 128), lambda i: (0, 0))],
        out_specs=[row, pl.BlockSpec((tm, d // 2), lambda i: (i, 0)), pl.BlockSpec((tm, 128), lambda i: (i, 0))],
        out_shape=[jax.ShapeDtypeStruct((m, d), F32), jax.ShapeDtypeStruct((m, d // 2), jnp.uint32),
                   jax.ShapeDtypeStruct((m, 128), F32)],
        compiler_params=_cparams(1),
        name="ln_route",
    )(x, a, g, b, w_router, b_router)


def _gather_rows_kernel(tok_ref, nt_ref, src_ref, o_ref, buf, sem):
    i = pl.program_id(0)
    n_tiles = nt_ref[0]

    def copy(tile, slot, r):
        return pltpu.make_async_copy(src_ref.at[pl.ds(tok_ref[tile * ROW_TILE + r], 1)],
                                     buf.at[slot, pl.ds(r, 1)], sem.at[slot])

    def issue(tile, slot):
        def body(r, c):
            copy(tile, slot, r).start()
            return c
        lax.fori_loop(0, ROW_TILE, body, 0, unroll=8)

    def drain(tile, slot):
        def body(r, c):
            copy(tile, slot, r).wait()
            return c
        lax.fori_loop(0, ROW_TILE, body, 0, unroll=8)

    @pl.when(jnp.logical_and(i == 0, n_tiles > 0))
    def _():
        issue(0, 0)

    @pl.when(i + 1 < n_tiles)
    def _():
        issue(i + 1, (i + 1) % 2)

    @pl.when(i < n_tiles)
    def _():
        drain(i, i % 2)
        o_ref[...] = buf[i % 2]

    @pl.when(i >= n_tiles)
    def _():
        o_ref[...] = jnp.zeros(o_ref.shape, o_ref.dtype)


def _gather_rows(src, row_token, n_tiles, r_max):
    d = src.shape[1]
    return pl.pallas_call(
        _gather_rows_kernel,
        grid_spec=pltpu.PrefetchScalarGridSpec(
            num_scalar_prefetch=2,
            grid=(r_max // ROW_TILE,),
            in_specs=[pl.BlockSpec(memory_space=pl.ANY)],
            out_specs=pl.BlockSpec((ROW_TILE, d), lambda i, tok, nt: (i, 0)),
            scratch_shapes=[pltpu.VMEM((2, ROW_TILE, d), src.dtype), pltpu.SemaphoreType.DMA((2,))],
        ),
        out_shape=jax.ShapeDtypeStruct((r_max, d), src.dtype),
        compiler_params=_cparams(1),
        name="moe_gather",
    )(row_token, n_tiles, src)


def _expert_changed(te_ref, i):
    prev = te_ref[jnp.maximum(i - 1, 0)]
    return jnp.logical_or(i == 0, te_ref[i] != prev)


def _moe_up_kernel(te_ref, nt_ref, x_ref, wg_ref, wu_ref, gate_ref, h_ref, wgb_ref, wub_ref):
    i = pl.program_id(1)

    @pl.when(_expert_changed(te_ref, i))
    def _():
        wgb_ref[...] = wg_ref[...].astype(BF16)
        wub_ref[...] = wu_ref[...].astype(BF16)

    @pl.when(i < nt_ref[0])
    def _():
        x_lo, x_hi = _unpack_bf16_pairs(x_ref[...])
        half = x_lo.shape[1]

        def dot_split(wb_ref):
            return (jnp.dot(x_lo, wb_ref[0:half, :], preferred_element_type=F32)
                    + jnp.dot(x_hi, wb_ref[half:2 * half, :], preferred_element_type=F32))

        a = dot_split(wgb_ref)
        u = dot_split(wub_ref)
        h_ref[...] = (a * jax.nn.sigmoid(a) * u * gate_ref[...]).astype(h_ref.dtype)

    @pl.when(i >= nt_ref[0])
    def _():
        h_ref[...] = jnp.zeros(h_ref.shape, h_ref.dtype)


def _moe_up(x_sorted, w_gate, w_up, row_gate, tile_expert, n_tiles, layer, tf):
    r_max = x_sorted.shape[0]
    d = w_gate.shape[2]
    t_max = r_max // ROW_TILE
    wspec = pl.BlockSpec((None, None, d, tf), lambda f, i, te, nt: (layer, te[i], 0, f))
    return pl.pallas_call(
        _moe_up_kernel,
        grid_spec=pltpu.PrefetchScalarGridSpec(
            num_scalar_prefetch=2,
            grid=(EXPERT_FF // tf, t_max),
            in_specs=[
                pl.BlockSpec((ROW_TILE, d // 2), lambda f, i, te, nt: (i, 0)),
                wspec, wspec,
                pl.BlockSpec((ROW_TILE, 1), lambda f, i, te, nt: (i, 0)),
            ],
            out_specs=pl.BlockSpec((ROW_TILE, tf), lambda f, i, te, nt: (i, f)),
            scratch_shapes=[pltpu.VMEM((d, tf), BF16), pltpu.VMEM((d, tf), BF16)],
        ),
        out_shape=jax.ShapeDtypeStruct((r_max, EXPERT_FF), BF16),
        compiler_params=_cparams(2),
        name="moe_up",
    )(tile_expert, n_tiles, x_sorted, w_gate, w_up, row_gate)


def _moe_down_kernel(te_ref, nt_ref, h_ref, wd_ref, y_ref, wdb_ref):
    i = pl.program_id(1)

    @pl.when(_expert_changed(te_ref, i))
    def _():
        wdb_ref[...] = wd_ref[...].astype(BF16)

    @pl.when(i < nt_ref[0])
    def _():
        y_ref[...] = jnp.dot(h_ref[...], wdb_ref[...], preferred_element_type=F32)

    @pl.when(i >= nt_ref[0])
    def _():
        y_ref[...] = jnp.zeros(y_ref.shape, y_ref.dtype)


def _moe_down(h, w_down, tile_expert, n_tiles, layer, tn):
    r_max, ff = h.shape
    d = w_down.shape[-1]
    t_max = r_max // ROW_TILE
    return pl.pallas_call(
        _moe_down_kernel,
        grid_spec=pltpu.PrefetchScalarGridSpec(
            num_scalar_prefetch=2,
            grid=(d // tn, t_max),
            in_specs=[
                pl.BlockSpec((ROW_TILE, ff), lambda n, i, te, nt: (i, 0)),
                pl.BlockSpec((None, None, ff, tn), lambda n, i, te, nt: (layer, te[i], 0, n)),
            ],
            out_specs=pl.BlockSpec((ROW_TILE, tn), lambda n, i, te, nt: (i, n)),
            scratch_shapes=[pltpu.VMEM((ff, tn), BF16)],
        ),
        out_shape=jax.ShapeDtypeStruct((r_max, d), F32),
        compiler_params=_cparams(2),
        name="moe_down",
    )(tile_expert, n_tiles, h, w_down)


COMBINE_ROWS = 128


def _combine_ln_kernel(pos_ref, x_ref, g_ref, b_ref, ys_ref, y_ref, yb_ref, buf_ref, sem):
    i = pl.program_id(0)

    def copy(tile, slot, r, k):
        return pltpu.make_async_copy(ys_ref.at[pl.ds(pos_ref[2 * (tile * COMBINE_ROWS + r) + k], 1)],
                                     buf_ref.at[slot, k, pl.ds(r, 1)], sem.at[slot])

    def issue(tile, slot):
        def body(r, c):
            copy(tile, slot, r, 0).start()
            copy(tile, slot, r, 1).start()
            return c
        lax.fori_loop(0, COMBINE_ROWS, body, 0, unroll=4)

    def drain(tile, slot):
        def body(r, c):
            copy(tile, slot, r, 0).wait()
            copy(tile, slot, r, 1).wait()
            return c
        lax.fori_loop(0, COMBINE_ROWS, body, 0, unroll=4)

    @pl.when(i == 0)
    def _():
        issue(0, 0)

    @pl.when(i + 1 < pl.num_programs(0))
    def _():
        issue(i + 1, (i + 1) % 2)

    slot = i % 2
    drain(i, slot)
    f = buf_ref[slot, 0] + buf_ref[slot, 1]
    y = _layer_norm_rows(DEEPNORM_ALPHA * x_ref[...] + f, g_ref[...], b_ref[...])
    y_ref[...] = y
    yb_ref[...] = y.astype(BF16)


def _combine_ln(x, y_sorted, pos, g, b):
    m, d = x.shape
    row = pl.BlockSpec((COMBINE_ROWS, d), lambda i, pos: (i, 0))
    vec = pl.BlockSpec((1, d), lambda i, pos: (0, 0))
    return pl.pallas_call(
        _combine_ln_kernel,
        grid_spec=pltpu.PrefetchScalarGridSpec(
            num_scalar_prefetch=1,
            grid=(m // COMBINE_ROWS,),
            in_specs=[row, vec, vec, pl.BlockSpec(memory_space=pl.ANY)],
            out_specs=[row, row],
            scratch_shapes=[pltpu.VMEM((2, 2, COMBINE_ROWS, d), F32), pltpu.SemaphoreType.DMA((2,))],
        ),
        out_shape=[jax.ShapeDtypeStruct((m, d), F32), jax.ShapeDtypeStruct((m, d), BF16)],
        compiler_params=_cparams(1),
        name="moe_combine_ln",
    )(pos, x, g, b, y_sorted)


def _dispatch(route, n_valid, r_max):
    e = route[:n_valid, 0:2].astype(jnp.int32).reshape(-1)
    w = route[:n_valid, 2:4].reshape(-1)
    onehot = (e[:, None] == jnp.arange(N_EXPERTS, dtype=jnp.int32)[None, :]).astype(jnp.int32)
    csum = jnp.cumsum(onehot, axis=0)
    counts = csum[-1]
    rank = jnp.sum(csum * onehot, axis=1) - 1
    tiles_per = (counts + ROW_TILE - 1) // ROW_TILE
    tile_end = jnp.cumsum(tiles_per)
    tile_start = tile_end - tiles_per
    pos = tile_start[e] * ROW_TILE + rank
    n_tiles = tile_end[-1]
    t_max = r_max // ROW_TILE
    tile_ids = jnp.arange(t_max, dtype=jnp.int32)
    tile_expert = jnp.sum((tile_ids[:, None] >= tile_end[None, :]).astype(jnp.int32), axis=1)
    last_expert = jnp.sum((n_tiles - 1 >= tile_end).astype(jnp.int32))
    tile_expert = jnp.where(tile_ids < n_tiles, tile_expert, last_expert).astype(jnp.int32)
    row_assign = jnp.zeros((r_max,), jnp.int32).at[pos].set(jnp.arange(2 * n_valid, dtype=jnp.int32))
    row_token = row_assign // 2
    row_gate = w[row_assign]
    return pos.astype(jnp.int32), row_token, row_gate[:, None], tile_expert, n_tiles.astype(jnp.int32)


KMEAN_PAGES = 8
PAGES_PER_BLOCK = MOBA_BLOCK // PAGE_SIZE


def _kmean_kernel(pt_ref, *refs):
    pages, o_ref = refs[:-1], refs[-1]
    for blk in range(KMEAN_PAGES // PAGES_PER_BLOCK):
        tot = jnp.sum(pages[2 * blk][...], axis=0) + jnp.sum(pages[2 * blk + 1][...], axis=0)
        o_ref[blk] = tot * (1.0 / MOBA_BLOCK)


def _kmean(cache_k, page_table, layer):
    db, n_pages = page_table.shape
    steps = n_pages // KMEAN_PAGES
    bps = KMEAN_PAGES // PAGES_PER_BLOCK

    def page_spec(kk):
        return pl.BlockSpec((None, None, PAGE_SIZE, MOBA_HEADS, HEAD_DIM),
                            lambda b, s, pt: (layer, pt[b, s * KMEAN_PAGES + kk], 0, 0, 0))

    return pl.pallas_call(
        _kmean_kernel,
        grid_spec=pltpu.PrefetchScalarGridSpec(
            num_scalar_prefetch=1,
            grid=(db, steps),
            in_specs=[page_spec(kk) for kk in range(KMEAN_PAGES)],
            out_specs=pl.BlockSpec((None, bps, MOBA_HEADS, HEAD_DIM), lambda b, s, pt: (b, s, 0, 0)),
        ),
        out_shape=jax.ShapeDtypeStruct((db, n_pages // PAGES_PER_BLOCK, MOBA_HEADS, HEAD_DIM), F32),
        compiler_params=_cparams(2),
        name="moba_sample_kmean",
    )(page_table, *([cache_k] * KMEAN_PAGES))


def _topk_kernel(km_ref, q_ref, idx_ref, keep_ref):
    gate = jnp.sum(km_ref[...] * q_ref[...][None], axis=-1, keepdims=True)
    nb = gate.shape[0]
    bid = lax.broadcasted_iota(jnp.int32, gate.shape, 0)
    for r in range(MOBA_TOPK):
        m = jnp.max(gate, axis=0, keepdims=True)
        first = jnp.min(jnp.where(gate == m, bid, nb), axis=0, keepdims=True)
        idx_ref[r] = first[0]
        keep_ref[r] = jnp.where(jnp.abs(m[0]) < jnp.inf, 1, 0)
        gate = jnp.where(bid == first, -jnp.inf, gate)


def _sample_topk(k_mean, q):
    db, nb = k_mean.shape[:2]
    out = jax.ShapeDtypeStruct((db, MOBA_TOPK, MOBA_HEADS, 1), jnp.int32)
    ospec = pl.BlockSpec((None, MOBA_TOPK, MOBA_HEADS, 1), lambda b: (b, 0, 0, 0))
    return pl.pallas_call(
        _topk_kernel,
        grid=(db,),
        in_specs=[pl.BlockSpec((None, nb, MOBA_HEADS, HEAD_DIM), lambda b: (b, 0, 0, 0)),
                  pl.BlockSpec((None, MOBA_HEADS, HEAD_DIM), lambda b: (b, 0, 0))],
        out_specs=[ospec, ospec],
        out_shape=[out, out],
        compiler_params=_cparams(1),
        name="moba_sample_topk",
    )(k_mean, q)


SEL_ROWS = MOBA_TOPK * MOBA_BLOCK


def _moba_sample_kernel(page_ref, keep_ref, q_ref, kn_ref, vn_ref, ck_ref, cv_ref, o_ref,
                        kbuf, vbuf, sem, *, layer):
    b = pl.program_id(0)
    n_sel_pages = MOBA_TOPK * PAGES_PER_BLOCK

    def copies(h, j):
        page = page_ref[(b * MOBA_HEADS + h) * n_sel_pages + j]
        dst = pl.ds(j * PAGE_SIZE, PAGE_SIZE)
        return (pltpu.make_async_copy(ck_ref.at[layer, page, :, h, :], kbuf.at[h, dst, :], sem.at[0]),
                pltpu.make_async_copy(cv_ref.at[layer, page, :, h, :], vbuf.at[h, dst, :], sem.at[1]))

    def start(t, c):
        ck, cv = copies(t // n_sel_pages, t % n_sel_pages)
        ck.start()
        cv.start()
        return c

    def wait(t, c):
        ck, cv = copies(t // n_sel_pages, t % n_sel_pages)
        ck.wait()
        cv.wait()
        return c

    lax.fori_loop(0, MOBA_HEADS * n_sel_pages, start, 0)
    lax.fori_loop(0, MOBA_HEADS * n_sel_pages, wait, 0)

    q = q_ref[...]
    s = jnp.sum(kbuf[...] * q[:, None, :], axis=-1, keepdims=True) * ATTN_SCALE
    rid = lax.broadcasted_iota(jnp.int32, s.shape, 1)
    keep = keep_ref[...]
    kept = jnp.zeros(s.shape, jnp.int32)
    for r in range(MOBA_TOPK):
        kept = jnp.where(rid // MOBA_BLOCK == r, keep[r][:, None, :], kept)
    s = jnp.where(kept > 0, s, -jnp.inf)
    s_new = jnp.sum(kn_ref[...] * q, axis=-1, keepdims=True)[:, None, :] * ATTN_SCALE
    m = jnp.maximum(jnp.max(s, axis=1, keepdims=True), s_new)
    p = jnp.exp(s - m)
    p_new = jnp.exp(s_new - m)
    denom = jnp.sum(p, axis=1, keepdims=True) + p_new
    num = jnp.sum(p * vbuf[...], axis=1, keepdims=True) + p_new * vn_ref[...][:, None, :]
    o_ref[...] = (num / denom)[:, 0, :]


def _moba_sample(q, k_new, v_new, cache_k, cache_v, sel_pages, keep, layer):
    db = q.shape[0]
    hd = pl.BlockSpec((None, MOBA_HEADS, HEAD_DIM), lambda b, pg: (b, 0, 0))
    return pl.pallas_call(
        functools.partial(_moba_sample_kernel, layer=layer),
        grid_spec=pltpu.PrefetchScalarGridSpec(
            num_scalar_prefetch=1,
            grid=(db,),
            in_specs=[pl.BlockSpec((None, MOBA_TOPK, MOBA_HEADS, 1), lambda b, pg: (b, 0, 0, 0)),
                      hd, hd, hd,
                      pl.BlockSpec(memory_space=pl.ANY), pl.BlockSpec(memory_space=pl.ANY)],
            out_specs=hd,
            scratch_shapes=[pltpu.VMEM((MOBA_HEADS, SEL_ROWS, HEAD_DIM), F32),
                            pltpu.VMEM((MOBA_HEADS, SEL_ROWS, HEAD_DIM), F32),
                            pltpu.SemaphoreType.DMA((2,))],
        ),
        out_shape=jax.ShapeDtypeStruct((db, MOBA_HEADS, HEAD_DIM), F32),
        compiler_params=_cparams(1),
        name="moba_sample_attend",
    )(sel_pages, keep, q, k_new, v_new, cache_k, cache_v)


DIFF_PAGES = 4


def _diff_sample_kernel(pt_ref, *refs, lam_init):
    kp = refs[:DIFF_PAGES]
    vp = refs[DIFF_PAGES:2 * DIFF_PAGES]
    (q_ref, kn_ref, vn_ref, lq1_ref, lk1_ref, lq2_ref, lk2_ref, g_ref, o_ref,
     m_ref, l_ref, acc_ref) = refs[2 * DIFF_PAGES:]
    step = pl.program_id(1)

    @pl.when(step == 0)
    def _():
        for c in range(2):
            s_new = jnp.sum(kn_ref[c] * q_ref[c], axis=-1, keepdims=True) * ATTN_SCALE
            m_ref[c] = s_new
            l_ref[c] = jnp.ones(s_new.shape, F32)
            acc_ref[c] = vn_ref[...]

    for c in range(2):
        qc = q_ref[c]
        s = jnp.concatenate(
            [jnp.sum(kp[p][:, pl.ds(c, DIFF_HEADS, stride=2), :] * qc[None], axis=-1, keepdims=True)
             for p in range(DIFF_PAGES)], axis=0) * ATTN_SCALE
        m_old = m_ref[c]
        m_new = jnp.maximum(m_old, jnp.max(s, axis=0))
        alpha = jnp.exp(m_old - m_new)
        pr = jnp.exp(s - m_new[None])
        l_ref[c] = alpha * l_ref[c] + jnp.sum(pr, axis=0)
        pv = jnp.zeros(acc_ref.shape[1:], F32)
        for p in range(DIFF_PAGES):
            pv = pv + jnp.sum(pr[p * PAGE_SIZE:(p + 1) * PAGE_SIZE] * vp[p][...], axis=0)
        acc_ref[c] = alpha * acc_ref[c] + pv
        m_ref[c] = m_new

    @pl.when(step == pl.num_programs(1) - 1)
    def _():
        lam = _lambda_value(lq1_ref, lk1_ref, lq2_ref, lk2_ref, lam_init)
        o = acc_ref[0] / l_ref[0] - lam * (acc_ref[1] / l_ref[1])
        o_ref[...] = _subln(o, g_ref[...], lam_init)


def _diff_sample(q, k_new, v_new, cache_k, cache_v, page_table, lams, g, layer, lam_init):
    db, n_pages = page_table.shape
    steps = n_pages // DIFF_PAGES

    def kspec(kk):
        return pl.BlockSpec((None, None, PAGE_SIZE, 2 * DIFF_HEADS, HEAD_DIM),
                            lambda b, s, pt: (layer, pt[b, s * DIFF_PAGES + kk], 0, 0, 0))

    def vspec(kk):
        return pl.BlockSpec((None, None, PAGE_SIZE, DIFF_HEADS, DIFF_V_DIM),
                            lambda b, s, pt: (layer, pt[b, s * DIFF_PAGES + kk], 0, 0, 0))

    qk = pl.BlockSpec((None, 2, DIFF_HEADS, HEAD_DIM), lambda b, s, pt: (b, 0, 0, 0))
    vo = pl.BlockSpec((None, DIFF_HEADS, DIFF_V_DIM), lambda b, s, pt: (b, 0, 0))
    vec = pl.BlockSpec((1, HEAD_DIM), lambda b, s, pt: (0, 0))
    return pl.pallas_call(
        functools.partial(_diff_sample_kernel, lam_init=lam_init),
        grid_spec=pltpu.PrefetchScalarGridSpec(
            num_scalar_prefetch=1,
            grid=(db, steps),
            in_specs=([kspec(kk) for kk in range(DIFF_PAGES)] + [vspec(kk) for kk in range(DIFF_PAGES)]
                      + [qk, qk, vo, vec, vec, vec, vec,
                         pl.BlockSpec((1, DIFF_V_DIM), lambda b, s, pt: (0, 0))]),
            out_specs=vo,
            scratch_shapes=[pltpu.VMEM((2, DIFF_HEADS, 1), F32), pltpu.VMEM((2, DIFF_HEADS, 1), F32),
                            pltpu.VMEM((2, DIFF_HEADS, DIFF_V_DIM), F32)],
        ),
        out_shape=jax.ShapeDtypeStruct((db, DIFF_HEADS, DIFF_V_DIM), F32),
        compiler_params=_cparams(2),
        name="diff_sample",
    )(page_table, *([cache_k] * DIFF_PAGES), *([cache_v] * DIFF_PAGES), q, k_new, v_new, *lams, g)


def _rope_tables(positions):
    half = HEAD_DIM // 2
    inv_freq = ROPE_THETA ** (-jnp.arange(half, dtype=F32) / half)
    ang = positions.astype(F32)[:, None] * inv_freq[None, :]
    cos, sin = jnp.cos(ang), jnp.sin(ang)
    return jnp.concatenate([cos, cos], axis=-1), jnp.concatenate([-sin, sin], axis=-1)


def kernel(x_prompt, x_sample, cache_moba_k, cache_moba_v, cache_diff_k, cache_diff_v, page_table, w_in, w_out, lambda_q1, lambda_k1, lambda_q2, lambda_k2, subln_g, ln1_g, ln1_b, w_router_group, b_router_group, w_router_expert, b_router_expert, w_exp_gate, w_exp_up, w_exp_down, ln2_g, ln2_b):
    batch, seq, d = x_prompt.shape
    db = x_sample.shape[0]
    depth = w_in.shape[0]
    n_pages = page_table.shape[1]
    past = n_pages * PAGE_SIZE
    m_prompt = batch * seq
    n_valid = m_prompt + db
    m_pad = -(-n_valid // (3 * ROW_TILE)) * (3 * ROW_TILE)
    tm_big = m_pad // 8
    r_max = ((2 * n_valid + N_EXPERTS * (ROW_TILE - 1)) // ROW_TILE + 1) * ROW_TILE

    x = jnp.concatenate([x_prompt.reshape(m_prompt, d), x_sample.reshape(db, d),
                         jnp.zeros((m_pad - n_valid, d), F32)], axis=0)
    xb = x.astype(BF16)
    positions = jnp.concatenate([jnp.tile(jnp.arange(seq, dtype=jnp.int32), batch),
                                 jnp.full((db,), past, jnp.int32),
                                 jnp.zeros((m_pad - n_valid,), jnp.int32)])
    cos, sin = _rope_tables(positions)

    cdk = cache_diff_k.reshape(cache_diff_k.shape[:3] + (2 * DIFF_HEADS, HEAD_DIM))

    outs = [[] for _ in range(8)]
    for l in range(depth):
        lam_init = 0.8 - 0.6 * math.exp(-0.3 * l)
        lams = (lambda_q1[l][None], lambda_k1[l][None], lambda_q2[l][None], lambda_k2[l][None])
        g_sub = subln_g[l][None]

        proj = functools.partial(_project, xb, w_in, l, cos=cos, sin=sin, tm=tm_big, tn=512)
        qm = proj(0 * GROUP_WIDTH, GROUP_WIDTH, rope=True)
        heads = functools.partial(proj, head_rows=m_prompt)
        km, km_h = heads(1 * GROUP_WIDTH, GROUP_WIDTH, rope=True, head_width=HEAD_DIM)
        vm, vm_h = heads(2 * GROUP_WIDTH, GROUP_WIDTH, rope=False, head_width=HEAD_DIM)
        qd = proj(3 * GROUP_WIDTH, GROUP_WIDTH, rope=True)
        kd, kd_h = heads(4 * GROUP_WIDTH, GROUP_WIDTH, rope=True, head_width=HEAD_DIM)
        vd, vd_h = heads(5 * GROUP_WIDTH, GROUP_WIDTH, rope=False, head_width=DIFF_V_DIM)

        om_p = _moba_prompt(qm, km, vm, batch, seq)
        od_p = _diff_prompt(qd, kd, vd, lams, g_sub, batch, seq, lam_init)

        sl = slice(m_prompt, n_valid)
        qm_s = qm[sl].reshape(db, MOBA_HEADS, HEAD_DIM)
        km_s = km[sl].reshape(db, MOBA_HEADS, HEAD_DIM)
        vm_s = vm[sl].reshape(db, MOBA_HEADS, HEAD_DIM)
        k_mean = _kmean(cache_moba_k, page_table, l)
        sel_idx, keep = _sample_topk(k_mean, qm_s)
        blk = jnp.transpose(sel_idx[..., 0], (0, 2, 1))
        logical = blk[..., None] * PAGES_PER_BLOCK + jnp.arange(PAGES_PER_BLOCK, dtype=jnp.int32)
        phys = jnp.take_along_axis(page_table[:, None, :], logical.reshape(db, MOBA_HEADS, -1), axis=2)
        om_s = _moba_sample(qm_s, km_s, vm_s, cache_moba_k, cache_moba_v,
                            phys.reshape(-1).astype(jnp.int32), keep, l)

        qd_s = jnp.transpose(qd[sl].reshape(db, DIFF_HEADS, 2, HEAD_DIM), (0, 2, 1, 3))
        kd_s = jnp.transpose(kd[sl].reshape(db, DIFF_HEADS, 2, HEAD_DIM), (0, 2, 1, 3))
        vd_s = vd[sl].reshape(db, DIFF_HEADS, DIFF_V_DIM)
        od_s = _diff_sample(qd_s, kd_s, vd_s, cdk, cache_diff_v, page_table, lams, g_sub, l, lam_init)

        cat = jnp.concatenate([
            jnp.concatenate([om_p, od_p], axis=1),
            jnp.concatenate([om_s.reshape(db, -1), od_s.reshape(db, -1)], axis=1).astype(BF16),
            jnp.zeros((m_pad - n_valid, d), BF16)], axis=0)
        a = _project(cat, w_out, l, 0, d, cos, sin, rope=False, tm=tm_big, tn=512)

        w_router = jnp.concatenate([w_router_group[l], w_router_expert[l],
                                    jnp.zeros((d, 128 - N_GROUPS - N_EXPERTS), F32)], axis=1)
        b_router = jnp.concatenate([b_router_group[l], b_router_expert[l],
                                    jnp.zeros((128 - N_GROUPS - N_EXPERTS,), F32)])[None]
        h1, h1_packed, route = _ln_route(x, a, ln1_g[l][None], ln1_b[l][None], w_router, b_router, ROW_TILE)

        pos, row_token, row_gate, tile_expert, n_tiles = _dispatch(route, n_valid, r_max)
        x_sorted = _gather_rows(h1_packed, row_token, n_tiles[None], r_max)
        hid = _moe_up(x_sorted, w_exp_gate, w_exp_up, row_gate, tile_expert, n_tiles[None], l, 512)
        y_sorted = _moe_down(hid, w_exp_down, tile_expert, n_tiles[None], l, 2048)
        pos_pad = jnp.concatenate([pos, jnp.zeros((2 * (m_pad - n_valid),), jnp.int32)])
        x, xb = _combine_ln(h1, y_sorted, pos_pad, ln2_g[l][None], ln2_b[l][None])

        outs[0].append(km_h.reshape(batch, seq, MOBA_HEADS, HEAD_DIM))
        outs[1].append(vm_h.reshape(batch, seq, MOBA_HEADS, HEAD_DIM))
        outs[2].append(kd_h.reshape(batch, seq, DIFF_HEADS, 2, HEAD_DIM))
        outs[3].append(vd_h.reshape(batch, seq, DIFF_HEADS, DIFF_V_DIM))
        outs[4].append(km[sl].reshape(db, 1, MOBA_HEADS, HEAD_DIM))
        outs[5].append(vm[sl].reshape(db, 1, MOBA_HEADS, HEAD_DIM))
        outs[6].append(kd[sl].reshape(db, 1, DIFF_HEADS, 2, HEAD_DIM))
        outs[7].append(vd[sl].reshape(db, 1, DIFF_HEADS, DIFF_V_DIM))

    y_prompt = x[:m_prompt].reshape(batch, seq, d)
    y_sample = x[m_prompt:n_valid].reshape(db, 1, d)
    return (y_prompt, y_sample) + tuple(jnp.stack(o) for o in outs)
```

```python
import functools
import math

import jax
import jax.numpy as jnp
from jax import lax
from jax.experimental import pallas as pl
from jax.experimental.pallas import tpu as pltpu

F32 = jnp.float32
BF16 = jnp.bfloat16
HIGHEST = lax.Precision.HIGHEST

D_MODEL = 4096
HEAD_DIM = 128
MOBA_HEADS = 16
DIFF_HEADS = 8
DIFF_V_DIM = 256
GROUP_WIDTH = 2048
MOBA_BLOCK = 256
MOBA_TOPK = 3
PAGE_SIZE = 128
ROPE_THETA = 10000.0
N_GROUPS = 4
EXPERTS_PER_GROUP = 8
N_EXPERTS = 32
EXPERT_FF = 1024
DEPTH = 2
DEEPNORM_ALPHA = (2 * DEPTH) ** 0.25
LN_EPS = 1e-5
SUBLN_EPS = 1e-5
ATTN_SCALE = HEAD_DIM ** -0.5

ROW_TILE = 256
VMEM_LIMIT = 56 * 1024 * 1024


def _cparams(n_axes):
    return pltpu.CompilerParams(dimension_semantics=("arbitrary",) * n_axes,
                                vmem_limit_bytes=VMEM_LIMIT)


def _proj_kernel(x_ref, w_ref, cos_ref, sin_ref, o_ref, *rest, rope, head_width, rows_last):
    j, i = pl.program_id(0), pl.program_id(1)
    wb_ref = rest[0] if head_width is None else rest[1]

    @pl.when(i == 0)
    def _():
        wb_ref[...] = w_ref[...].astype(BF16)

    acc = jnp.dot(x_ref[...], wb_ref[...], preferred_element_type=F32)
    if rope:
        cos = cos_ref[...]
        sin = sin_ref[...]
        for h in range(acc.shape[1] // HEAD_DIM):
            z = acc[:, h * HEAD_DIM:(h + 1) * HEAD_DIM]
            o_ref[:, h * HEAD_DIM:(h + 1) * HEAD_DIM] = (
                z * cos + pltpu.roll(z, HEAD_DIM // 2, 1) * sin)
    else:
        o_ref[...] = acc

    if head_width is not None:
        heads_ref, _, stage_ref, sem = rest
        tm, tn = o_ref.shape
        heads_per_block = tn // head_width
        n_i = pl.num_programs(1)
        step = j * n_i + i
        last = pl.num_programs(0) * n_i - 1
        slot = step % 2

        def copies(s, tile, n_rows):
            return [pltpu.make_async_copy(
                stage_ref.at[s, 0:n_rows, hh * head_width:(hh + 1) * head_width],
                heads_ref.at[pl.ds(tile * tm, n_rows), j * heads_per_block + hh, :],
                sem.at[s]) for hh in range(heads_per_block)]

        def for_tile(s, tile, action):
            @pl.when(tile == n_i - 1)
            def _():
                for c in copies(s, tile, rows_last):
                    action(c)

            @pl.when(tile != n_i - 1)
            def _():
                for c in copies(s, tile, tm):
                    action(c)

        @pl.when(step >= 2)
        def _():
            for_tile(slot, lax.rem(step - 2, n_i), lambda c: c.wait())

        stage_ref[slot] = o_ref[...]
        for_tile(slot, i, lambda c: c.start())

        @pl.when(step == last)
        def _():
            for_tile(slot, i, lambda c: c.wait())

        @pl.when(jnp.logical_and(step == last, step >= 1))
        def _():
            for_tile(1 - slot, lax.rem(step - 1, n_i), lambda c: c.wait())


def _project(x_bf16, w, layer, col0, n_cols, cos, sin, *, rope, tm, tn, head_width=None, head_rows=None):
    m, k = x_bf16.shape
    jb0 = col0 // tn
    out_specs = [pl.BlockSpec((tm, tn), lambda j, i: (i, j))]
    out_shape = [jax.ShapeDtypeStruct((m, n_cols), F32)]
    scratch = [pltpu.VMEM((k, tn), BF16)]
    rows_last = None
    if head_width is not None:
        rows_last = head_rows - (m // tm - 1) * tm
        assert 0 < rows_last <= tm and rows_last % 8 == 0, (head_rows, tm)
        out_specs.append(pl.BlockSpec(memory_space=pl.ANY))
        out_shape.append(jax.ShapeDtypeStruct((head_rows, n_cols // head_width, head_width), F32))
        scratch += [pltpu.VMEM((2, tm, tn), F32), pltpu.SemaphoreType.DMA((2,))]
    outs = pl.pallas_call(
        functools.partial(_proj_kernel, rope=rope, head_width=head_width, rows_last=rows_last),
        grid=(n_cols // tn, m // tm),
        in_specs=[
            pl.BlockSpec((tm, k), lambda j, i: (i, 0)),
            pl.BlockSpec((None, k, tn), lambda j, i: (layer, 0, jb0 + j)),
            pl.BlockSpec((tm, HEAD_DIM), lambda j, i: (i, 0)),
            pl.BlockSpec((tm, HEAD_DIM), lambda j, i: (i, 0)),
        ],
        out_specs=out_specs,
        out_shape=out_shape,
        scratch_shapes=scratch,
        compiler_params=_cparams(2),
        name=("proj_rope" if rope else "proj") + ("" if head_width is None else "_heads"),
    )(x_bf16, w, cos, sin)
    return outs[0] if head_width is None else outs


def _nt_dot(a, b, **kw):
    return lax.dot_general(a, b, (((1,), (1,)), ((), ())), preferred_element_type=F32, **kw)


def _causal_mask(n):
    r_id = lax.broadcasted_iota(jnp.int32, (n, n), 0)
    c_id = lax.broadcasted_iota(jnp.int32, (n, n), 1)
    return c_id <= r_id


def _softmax_parts(s):
    m = jnp.max(s, axis=1, keepdims=True)
    p = jnp.exp(s - m)
    return p, jnp.sum(p, axis=1, keepdims=True)


def _moba_select(gate, qi):
    blk = lax.broadcasted_iota(jnp.int32, gate.shape, 1)
    gate = jnp.where(blk < qi, gate, -jnp.inf)
    finite = jnp.where(jnp.abs(gate) < jnp.inf, 1.0, 0.0)
    if qi <= MOBA_TOPK:
        return finite
    rank = jnp.zeros(gate.shape, jnp.int32)
    for n2 in range(qi):
        col = gate[:, n2:n2 + 1]
        rank = rank + jnp.where(col > gate, 1, jnp.where(col == gate, jnp.where(n2 < blk, 1, 0), 0))
    return jnp.where(rank < MOBA_TOPK, finite, 0.0)


def _moba_prompt_kernel(q_ref, k_ref, v_ref, o_ref, *, n_blocks):
    k = k_ref[...]
    kb = k.astype(BF16)
    vb = v_ref[...].astype(BF16)
    k_mean = jnp.mean(k.reshape(n_blocks, MOBA_BLOCK, HEAD_DIM), axis=1)
    gate_all = _nt_dot(q_ref[...], k_mean, precision=HIGHEST)
    causal = _causal_mask(MOBA_BLOCK)
    for qi in range(n_blocks):
        rows = slice(qi * MOBA_BLOCK, (qi + 1) * MOBA_BLOCK)
        n_keys = (qi + 1) * MOBA_BLOCK
        qb = q_ref[rows, :].astype(BF16)
        s = _nt_dot(qb, kb[:n_keys]) * ATTN_SCALE
        parts = []
        if qi:
            sel = _moba_select(gate_all[rows], qi)
            parts = [jnp.where(sel[:, j:j + 1] > 0.0, s[:, j * MOBA_BLOCK:(j + 1) * MOBA_BLOCK], -jnp.inf)
                     for j in range(qi)]
        parts.append(jnp.where(causal, s[:, qi * MOBA_BLOCK:], -jnp.inf))
        p, l = _softmax_parts(jnp.concatenate(parts, axis=1) if qi else parts[0])
        o = jnp.dot(p.astype(BF16), vb[:n_keys], preferred_element_type=F32) / l
        o_ref[rows, :] = o.astype(o_ref.dtype)


def _moba_prompt(q, k, v, batch, seq):
    blk = pl.BlockSpec((seq, HEAD_DIM), lambda b, h: (b, h))
    return pl.pallas_call(
        functools.partial(_moba_prompt_kernel, n_blocks=seq // MOBA_BLOCK),
        grid=(batch, MOBA_HEADS),
        in_specs=[blk, blk, blk],
        out_specs=blk,
        out_shape=jax.ShapeDtypeStruct((batch * seq, MOBA_HEADS * HEAD_DIM), BF16),
        compiler_params=_cparams(2),
        name="moba_prompt",
    )(q, k, v)


def _lambda_value(lq1_ref, lk1_ref, lq2_ref, lk2_ref, lam_init):
    a = jnp.sum(lq1_ref[...] * lk1_ref[...], axis=1, keepdims=True)
    b = jnp.sum(lq2_ref[...] * lk2_ref[...], axis=1, keepdims=True)
    return jnp.exp(a) - jnp.exp(b) + lam_init


def _subln(o, g, lam_init):
    o = o * lax.rsqrt(jnp.mean(jnp.square(o), axis=-1, keepdims=True) + SUBLN_EPS)
    return o * g * (1.0 - lam_init)


DIFF_TQ = 256


def _diff_prompt_kernel(q_ref, k_ref, v_ref, lq1_ref, lk1_ref, lq2_ref, lk2_ref, g_ref, o_ref,
                        *, n_blocks, lam_init):
    lam = _lambda_value(lq1_ref, lk1_ref, lq2_ref, lk2_ref, lam_init)
    kb = k_ref[...].astype(BF16)
    vb = v_ref[...].astype(BF16)
    g = g_ref[...]
    causal = _causal_mask(DIFF_TQ)
    for qi in range(n_blocks):
        rows = slice(qi * DIFF_TQ, (qi + 1) * DIFF_TQ)
        n_keys = (qi + 1) * DIFF_TQ
        probs = []
        for c in range(2):
            cs = slice(c * HEAD_DIM, (c + 1) * HEAD_DIM)
            s = _nt_dot(q_ref[rows, cs].astype(BF16), kb[:n_keys, cs]) * ATTN_SCALE
            diag = jnp.where(causal, s[:, qi * DIFF_TQ:], -jnp.inf)
            s = jnp.concatenate([s[:, :qi * DIFF_TQ], diag], axis=1) if qi else diag
            probs.append(_softmax_parts(s))
        (p1, l1), (p2, l2) = probs
        a = p1 * (1.0 / l1) - p2 * (lam / l2)
        o = jnp.dot(a.astype(BF16), vb[:n_keys], preferred_element_type=F32)
        o_ref[rows, :] = _subln(o, g, lam_init).astype(o_ref.dtype)


def _diff_prompt(q, k, v, lams, g, batch, seq, lam_init):
    blk = pl.BlockSpec((seq, DIFF_V_DIM), lambda b, h: (b, h))
    vec = pl.BlockSpec((1, HEAD_DIM), lambda b, h: (0, 0))
    return pl.pallas_call(
        functools.partial(_diff_prompt_kernel, n_blocks=seq // DIFF_TQ, lam_init=lam_init),
        grid=(batch, DIFF_HEADS),
        in_specs=[blk, blk, blk, vec, vec, vec, vec, pl.BlockSpec((1, DIFF_V_DIM), lambda b, h: (0, 0))],
        out_specs=blk,
        out_shape=jax.ShapeDtypeStruct((batch * seq, DIFF_HEADS * DIFF_V_DIM), BF16),
        compiler_params=_cparams(2),
        name="diff_prompt",
    )(q, k, v, *lams, g)


def _layer_norm_rows(z, g, b):
    mu = jnp.mean(z, axis=-1, keepdims=True)
    zc = z - mu
    var = jnp.mean(zc * zc, axis=-1, keepdims=True)
    return zc * lax.rsqrt(var + LN_EPS) * g + b


def _route(logits):
    lane = lax.broadcasted_iota(jnp.int32, logits.shape, 1)
    lane_f = lane.astype(F32)
    big = float(logits.shape[1])
    lg = jnp.where(lane < N_GROUPS, logits, -jnp.inf)
    mg = jnp.max(lg, axis=1, keepdims=True)
    sg = jnp.sum(jnp.exp(lg - mg), axis=1, keepdims=True)
    g_val = 1.0 / sg
    g_idx = jnp.min(jnp.where(lg == mg, lane_f, big), axis=1, keepdims=True)
    lo = N_GROUPS + EXPERTS_PER_GROUP * g_idx
    in_grp = jnp.where(lane_f >= lo, jnp.where(lane_f < lo + EXPERTS_PER_GROUP, 1.0, 0.0), 0.0)
    le = jnp.where(in_grp > 0.0, logits, -jnp.inf)
    me = jnp.max(le, axis=1, keepdims=True)
    se = jnp.sum(jnp.exp(le - me), axis=1, keepdims=True)
    e1 = jnp.min(jnp.where(le == me, lane_f, big), axis=1, keepdims=True)
    le2 = jnp.where(lane_f == e1, -jnp.inf, le)
    m2 = jnp.max(le2, axis=1, keepdims=True)
    e2 = jnp.min(jnp.where(le2 == m2, lane_f, big), axis=1, keepdims=True)
    p1 = 1.0 / se
    p2 = jnp.exp(m2 - me) / se
    w1 = g_val * p1 / (p1 + p2)
    w2 = g_val * p2 / (p1 + p2)
    out = jnp.where(lane == 0, e1 - N_GROUPS,
                    jnp.where(lane == 1, e2 - N_GROUPS,
                              jnp.where(lane == 2, w1, jnp.where(lane == 3, w2, 0.0))))
    return out


def _pack_bf16_pairs(y):
    half = y.shape[1] // 2
    lo = lax.bitcast_convert_type(y[:, :half].astype(BF16).astype(F32), jnp.uint32)
    hi = lax.bitcast_convert_type(y[:, half:].astype(BF16).astype(F32), jnp.uint32)
    return lax.shift_right_logical(lo, jnp.uint32(16)) | (hi & jnp.uint32(0xFFFF0000))


def _unpack_bf16_pairs(u):
    lo = lax.bitcast_convert_type(lax.shift_left(u, jnp.uint32(16)), F32)
    hi = lax.bitcast_convert_type(u & jnp.uint32(0xFFFF0000), F32)
    return lo.astype(BF16), hi.astype(BF16)


def _ln_route_kernel(x_ref, a_ref, g_ref, b_ref, wr_ref, br_ref, y_ref, yp_ref, r_ref):
    y = _layer_norm_rows(DEEPNORM_ALPHA * x_ref[...] + a_ref[...], g_ref[...], b_ref[...])
    y_ref[...] = y
    yp_ref[...] = _pack_bf16_pairs(y)
    logits = jnp.dot(y, wr_ref[...], precision=HIGHEST, preferred_element_type=F32) + br_ref[...]
    r_ref[...] = _route(logits)


def _ln_route(x, a, g, b, w_router, b_router, tm):
    m, d = x.shape
    row = pl.BlockSpec((tm, d), lambda i: (i, 0))
    vec = pl.BlockSpec((1, d), lambda i: (0, 0))
    return pl.pallas_call(
        _ln_route_kernel,
        grid=(m // tm,),
        in_specs=[row, row, vec, vec,
                  pl.BlockSpec((d, 128), lambda i: (0, 0)),
                  pl.BlockSpec((1, 128), lambda i: (0, 0))],
        out_specs=[row, pl.BlockSpec((tm, d // 2), lambda i: (i, 0)), pl.BlockSpec((tm, 128), lambda i: (i, 0))],
        out_shape=[jax.ShapeDtypeStruct((m, d), F32), jax.ShapeDtypeStruct((m, d // 2), jnp.uint32),
                   jax.ShapeDtypeStruct((m, 128), F32)],
        compiler_params=_cparams(1),
        name="ln_route",
    )(x, a, g, b, w_router, b_router)


def _gather_rows_kernel(tok_ref, nt_ref, src_ref, o_ref, buf, sem):
    i = pl.program_id(0)
    n_tiles = nt_ref[0]

    def copy(tile, slot, r):
        return pltpu.make_async_copy(src_ref.at[pl.ds(tok_ref[tile * ROW_TILE + r], 1)],
                                     buf.at[slot, pl.ds(r, 1)], sem.at[slot])

    def issue(tile, slot):
        def body(r, c):
            copy(tile, slot, r).start()
            return c
        lax.fori_loop(0, ROW_TILE, body, 0, unroll=8)

    def drain(tile, slot):
        def body(r, c):
            copy(tile, slot, r).wait()
            return c
        lax.fori_loop(0, ROW_TILE, body, 0, unroll=8)

    @pl.when(jnp.logical_and(i == 0, n_tiles > 0))
    def _():
        issue(0, 0)

    @pl.when(i + 1 < n_tiles)
    def _():
        issue(i + 1, (i + 1) % 2)

    @pl.when(i < n_tiles)
    def _():
        drain(i, i % 2)
        o_ref[...] = buf[i % 2]

    @pl.when(i >= n_tiles)
    def _():
        o_ref[...] = jnp.zeros(o_ref.shape, o_ref.dtype)


def _gather_rows(src, row_token, n_tiles, r_max):
    d = src.shape[1]
    return pl.pallas_call(
        _gather_rows_kernel,
        grid_spec=pltpu.PrefetchScalarGridSpec(
            num_scalar_prefetch=2,
            grid=(r_max // ROW_TILE,),
            in_specs=[pl.BlockSpec(memory_space=pl.ANY)],
            out_specs=pl.BlockSpec((ROW_TILE, d), lambda i, tok, nt: (i, 0)),
            scratch_shapes=[pltpu.VMEM((2, ROW_TILE, d), src.dtype), pltpu.SemaphoreType.DMA((2,))],
        ),
        out_shape=jax.ShapeDtypeStruct((r_max, d), src.dtype),
        compiler_params=_cparams(1),
        name="moe_gather",
    )(row_token, n_tiles, src)


def _expert_changed(te_ref, i):
    prev = te_ref[jnp.maximum(i - 1, 0)]
    return jnp.logical_or(i == 0, te_ref[i] != prev)


def _moe_up_kernel(te_ref, nt_ref, x_ref, wg_ref, wu_ref, gate_ref, h_ref, wgb_ref, wub_ref):
    i = pl.program_id(1)

    @pl.when(_expert_changed(te_ref, i))
    def _():
        wgb_ref[...] = wg_ref[...].astype(BF16)
        wub_ref[...] = wu_ref[...].astype(BF16)

    @pl.when(i < nt_ref[0])
    def _():
        x_lo, x_hi = _unpack_bf16_pairs(x_ref[...])
        half = x_lo.shape[1]

        def dot_split(wb_ref):
            return (jnp.dot(x_lo, wb_ref[0:half, :], preferred_element_type=F32)
                    + jnp.dot(x_hi, wb_ref[half:2 * half, :], preferred_element_type=F32))

        a = dot_split(wgb_ref)
        u = dot_split(wub_ref)
        h_ref[...] = (a * jax.nn.sigmoid(a) * u * gate_ref[...]).astype(h_ref.dtype)

    @pl.when(i >= nt_ref[0])
    def _():
        h_ref[...] = jnp.zeros(h_ref.shape, h_ref.dtype)


def _moe_up(x_sorted, w_gate, w_up, row_gate, tile_expert, n_tiles, layer, tf):
    r_max = x_sorted.shape[0]
    d = w_gate.shape[2]
    t_max = r_max // ROW_TILE
    wspec = pl.BlockSpec((None, None, d, tf), lambda f, i, te, nt: (layer, te[i], 0, f))
    return pl.pallas_call(
        _moe_up_kernel,
        grid_spec=pltpu.PrefetchScalarGridSpec(
            num_scalar_prefetch=2,
            grid=(EXPERT_FF // tf, t_max),
            in_specs=[
                pl.BlockSpec((ROW_TILE, d // 2), lambda f, i, te, nt: (i, 0)),
                wspec, wspec,
                pl.BlockSpec((ROW_TILE, 1), lambda f, i, te, nt: (i, 0)),
            ],
            out_specs=pl.BlockSpec((ROW_TILE, tf), lambda f, i, te, nt: (i, f)),
            scratch_shapes=[pltpu.VMEM((d, tf), BF16), pltpu.VMEM((d, tf), BF16)],
        ),
        out_shape=jax.ShapeDtypeStruct((r_max, EXPERT_FF), BF16),
        compiler_params=_cparams(2),
        name="moe_up",
    )(tile_expert, n_tiles, x_sorted, w_gate, w_up, row_gate)


def _moe_down_kernel(te_ref, nt_ref, h_ref, wd_ref, y_ref, wdb_ref):
    i = pl.program_id(1)

    @pl.when(_expert_changed(te_ref, i))
    def _():
        wdb_ref[...] = wd_ref[...].astype(BF16)

    @pl.when(i < nt_ref[0])
    def _():
        y_ref[...] = jnp.dot(h_ref[...], wdb_ref[...], preferred_element_type=F32)

    @pl.when(i >= nt_ref[0])
    def _():
        y_ref[...] = jnp.zeros(y_ref.shape, y_ref.dtype)


def _moe_down(h, w_down, tile_expert, n_tiles, layer, tn):
    r_max, ff = h.shape
    d = w_down.shape[-1]
    t_max = r_max // ROW_TILE
    return pl.pallas_call(
        _moe_down_kernel,
        grid_spec=pltpu.PrefetchScalarGridSpec(
            num_scalar_prefetch=2,
            grid=(d // tn, t_max),
            in_specs=[
                pl.BlockSpec((ROW_TILE, ff), lambda n, i, te, nt: (i, 0)),
                pl.BlockSpec((None, None, ff, tn), lambda n, i, te, nt: (layer, te[i], 0, n)),
            ],
            out_specs=pl.BlockSpec((ROW_TILE, tn), lambda n, i, te, nt: (i, n)),
            scratch_shapes=[pltpu.VMEM((ff, tn), BF16)],
        ),
        out_shape=jax.ShapeDtypeStruct((r_max, d), F32),
        compiler_params=_cparams(2),
        name="moe_down",
    )(tile_expert, n_tiles, h, w_down)


COMBINE_ROWS = 128


def _combine_ln_kernel(pos_ref, x_ref, g_ref, b_ref, ys_ref, y_ref, yb_ref, buf_ref, sem):
    i = pl.program_id(0)

    def copy(tile, slot, r, k):
        return pltpu.make_async_copy(ys_ref.at[pl.ds(pos_ref[2 * (tile * COMBINE_ROWS + r) + k], 1)],
                                     buf_ref.at[slot, k, pl.ds(r, 1)], sem.at[slot])

    def issue(tile, slot):
        def body(r, c):
            copy(tile, slot, r, 0).start()
            copy(tile, slot, r, 1).start()
            return c
        lax.fori_loop(0, COMBINE_ROWS, body, 0, unroll=4)

    def drain(tile, slot):
        def body(r, c):
            copy(tile, slot, r, 0).wait()
            copy(tile, slot, r, 1).wait()
            return c
        lax.fori_loop(0, COMBINE_ROWS, body, 0, unroll=4)

    @pl.when(i == 0)
    def _():
        issue(0, 0)

    @pl.when(i + 1 < pl.num_programs(0))
    def _():
        issue(i + 1, (i + 1) % 2)

    slot = i % 2
    drain(i, slot)
    f = buf_ref[slot, 0] + buf_ref[slot, 1]
    y = _layer_norm_rows(DEEPNORM_ALPHA * x_ref[...] + f, g_ref[...], b_ref[...])
    y_ref[...] = y
    yb_ref[...] = y.astype(BF16)


def _combine_ln(x, y_sorted, pos, g, b):
    m, d = x.shape
    row = pl.BlockSpec((COMBINE_ROWS, d), lambda i, pos: (i, 0))
    vec = pl.BlockSpec((1, d), lambda i, pos: (0, 0))
    return pl.pallas_call(
        _combine_ln_kernel,
        grid_spec=pltpu.PrefetchScalarGridSpec(
            num_scalar_prefetch=1,
            grid=(m // COMBINE_ROWS,),
            in_specs=[row, vec, vec, pl.BlockSpec(memory_space=pl.ANY)],
            out_specs=[row, row],
            scratch_shapes=[pltpu.VMEM((2, 2, COMBINE_ROWS, d), F32), pltpu.SemaphoreType.DMA((2,))],
        ),
        out_shape=[jax.ShapeDtypeStruct((m, d), F32), jax.ShapeDtypeStruct((m, d), BF16)],
        compiler_params=_cparams(1),
        name="moe_combine_ln",
    )(pos, x, g, b, y_sorted)


def _dispatch(route, n_valid, r_max):
    e = route[:n_valid, 0:2].astype(jnp.int32).reshape(-1)
    w = route[:n_valid, 2:4].reshape(-1)
    onehot = (e[:, None] == jnp.arange(N_EXPERTS, dtype=jnp.int32)[None, :]).astype(jnp.int32)
    csum = jnp.cumsum(onehot, axis=0)
    counts = csum[-1]
    rank = jnp.sum(csum * onehot, axis=1) - 1
    tiles_per = (counts + ROW_TILE - 1) // ROW_TILE
    tile_end = jnp.cumsum(tiles_per)
    tile_start = tile_end - tiles_per
    pos = tile_start[e] * ROW_TILE + rank
    n_tiles = tile_end[-1]
    t_max = r_max // ROW_TILE
    tile_ids = jnp.arange(t_max, dtype=jnp.int32)
    tile_expert = jnp.sum((tile_ids[:, None] >= tile_end[None, :]).astype(jnp.int32), axis=1)
    last_expert = jnp.sum((n_tiles - 1 >= tile_end).astype(jnp.int32))
    tile_expert = jnp.where(tile_ids < n_tiles, tile_expert, last_expert).astype(jnp.int32)
    row_assign = jnp.zeros((r_max,), jnp.int32).at[pos].set(jnp.arange(2 * n_valid, dtype=jnp.int32))
    row_token = row_assign // 2
    row_gate = w[row_assign]
    return pos.astype(jnp.int32), row_token, row_gate[:, None], tile_expert, n_tiles.astype(jnp.int32)


KMEAN_PAGES = 8
PAGES_PER_BLOCK = MOBA_BLOCK // PAGE_SIZE


def _kmean_kernel(pt_ref, *refs):
    pages, o_ref = refs[:-1], refs[-1]
    for blk in range(KMEAN_PAGES // PAGES_PER_BLOCK):
        tot = jnp.sum(pages[2 * blk][...], axis=0) + jnp.sum(pages[2 * blk + 1][...], axis=0)
        o_ref[blk] = tot * (1.0 / MOBA_BLOCK)


def _kmean(cache_k, page_table, layer):
    db, n_pages = page_table.shape
    steps = n_pages // KMEAN_PAGES
    bps = KMEAN_PAGES // PAGES_PER_BLOCK

    def page_spec(kk):
        return pl.BlockSpec((None, None, PAGE_SIZE, MOBA_HEADS, HEAD_DIM),
                            lambda b, s, pt: (layer, pt[b, s * KMEAN_PAGES + kk], 0, 0, 0))

    return pl.pallas_call(
        _kmean_kernel,
        grid_spec=pltpu.PrefetchScalarGridSpec(
            num_scalar_prefetch=1,
            grid=(db, steps),
            in_specs=[page_spec(kk) for kk in range(KMEAN_PAGES)],
            out_specs=pl.BlockSpec((None, bps, MOBA_HEADS, HEAD_DIM), lambda b, s, pt: (b, s, 0, 0)),
        ),
        out_shape=jax.ShapeDtypeStruct((db, n_pages // PAGES_PER_BLOCK, MOBA_HEADS, HEAD_DIM), F32),
        compiler_params=_cparams(2),
        name="moba_sample_kmean",
    )(page_table, *([cache_k] * KMEAN_PAGES))


def _topk_kernel(km_ref, q_ref, idx_ref, keep_ref):
    gate = jnp.sum(km_ref[...] * q_ref[...][None], axis=-1, keepdims=True)
    nb = gate.shape[0]
    bid = lax.broadcasted_iota(jnp.int32, gate.shape, 0)
    for r in range(MOBA_TOPK):
        m = jnp.max(gate, axis=0, keepdims=True)
        first = jnp.min(jnp.where(gate == m, bid, nb), axis=0, keepdims=True)
        idx_ref[r] = first[0]
        keep_ref[r] = jnp.where(jnp.abs(m[0]) < jnp.inf, 1, 0)
        gate = jnp.where(bid == first, -jnp.inf, gate)


def _sample_topk(k_mean, q):
    db, nb = k_mean.shape[:2]
    out = jax.ShapeDtypeStruct((db, MOBA_TOPK, MOBA_HEADS, 1), jnp.int32)
    ospec = pl.BlockSpec((None, MOBA_TOPK, MOBA_HEADS, 1), lambda b: (b, 0, 0, 0))
    return pl.pallas_call(
        _topk_kernel,
        grid=(db,),
        in_specs=[pl.BlockSpec((None, nb, MOBA_HEADS, HEAD_DIM), lambda b: (b, 0, 0, 0)),
                  pl.BlockSpec((None, MOBA_HEADS, HEAD_DIM), lambda b: (b, 0, 0))],
        out_specs=[ospec, ospec],
        out_shape=[out, out],
        compiler_params=_cparams(1),
        name="moba_sample_topk",
    )(k_mean, q)


SEL_ROWS = MOBA_TOPK * MOBA_BLOCK


def _moba_sample_kernel(page_ref, keep_ref, q_ref, kn_ref, vn_ref, ck_ref, cv_ref, o_ref,
                        kbuf, vbuf, sem, *, layer):
    b = pl.program_id(0)
    n_sel_pages = MOBA_TOPK * PAGES_PER_BLOCK

    def copies(h, j):
        page = page_ref[(b * MOBA_HEADS + h) * n_sel_pages + j]
        dst = pl.ds(j * PAGE_SIZE, PAGE_SIZE)
        return (pltpu.make_async_copy(ck_ref.at[layer, page, :, h, :], kbuf.at[h, dst, :], sem.at[0]),
                pltpu.make_async_copy(cv_ref.at[layer, page, :, h, :], vbuf.at[h, dst, :], sem.at[1]))

    def start(t, c):
        ck, cv = copies(t // n_sel_pages, t % n_sel_pages)
        ck.start()
        cv.start()
        return c

    def wait(t, c):
        ck, cv = copies(t // n_sel_pages, t % n_sel_pages)
        ck.wait()
        cv.wait()
        return c

    lax.fori_loop(0, MOBA_HEADS * n_sel_pages, start, 0)
    lax.fori_loop(0, MOBA_HEADS * n_sel_pages, wait, 0)

    q = q_ref[...]
    s = jnp.sum(kbuf[...] * q[:, None, :], axis=-1, keepdims=True) * ATTN_SCALE
    rid = lax.broadcasted_iota(jnp.int32, s.shape, 1)
    keep = keep_ref[...]
    kept = jnp.zeros(s.shape, jnp.int32)
    for r in range(MOBA_TOPK):
        kept = jnp.where(rid // MOBA_BLOCK == r, keep[r][:, None, :], kept)
    s = jnp.where(kept > 0, s, -jnp.inf)
    s_new = jnp.sum(kn_ref[...] * q, axis=-1, keepdims=True)[:, None, :] * ATTN_SCALE
    m = jnp.maximum(jnp.max(s, axis=1, keepdims=True), s_new)
    p = jnp.exp(s - m)
    p_new = jnp.exp(s_new - m)
    denom = jnp.sum(p, axis=1, keepdims=True) + p_new
    num = jnp.sum(p * vbuf[...], axis=1, keepdims=True) + p_new * vn_ref[...][:, None, :]
    o_ref[...] = (num / denom)[:, 0, :]


def _moba_sample(q, k_new, v_new, cache_k, cache_v, sel_pages, keep, layer):
    db = q.shape[0]
    hd = pl.BlockSpec((None, MOBA_HEADS, HEAD_DIM), lambda b, pg: (b, 0, 0))
    return pl.pallas_call(
        functools.partial(_moba_sample_kernel, layer=layer),
        grid_spec=pltpu.PrefetchScalarGridSpec(
            num_scalar_prefetch=1,
            grid=(db,),
            in_specs=[pl.BlockSpec((None, MOBA_TOPK, MOBA_HEADS, 1), lambda b, pg: (b, 0, 0, 0)),
                      hd, hd, hd,
                      pl.BlockSpec(memory_space=pl.ANY), pl.BlockSpec(memory_space=pl.ANY)],
            out_specs=hd,
            scratch_shapes=[pltpu.VMEM((MOBA_HEADS, SEL_ROWS, HEAD_DIM), F32),
                            pltpu.VMEM((MOBA_HEADS, SEL_ROWS, HEAD_DIM), F32),
                            pltpu.SemaphoreType.DMA((2,))],
        ),
        out_shape=jax.ShapeDtypeStruct((db, MOBA_HEADS, HEAD_DIM), F32),
        compiler_params=_cparams(1),
        name="moba_sample_attend",
    )(sel_pages, keep, q, k_new, v_new, cache_k, cache_v)


DIFF_PAGES = 8


def _diff_sample_kernel(pt_ref, *refs, lam_init):
    kp = refs[:DIFF_PAGES]
    vp = refs[DIFF_PAGES:2 * DIFF_PAGES]
    (q_ref, kn_ref, vn_ref, lq1_ref, lk1_ref, lq2_ref, lk2_ref, g_ref, o_ref,
     m_ref, l_ref, acc_ref) = refs[2 * DIFF_PAGES:]
    step = pl.program_id(1)

    @pl.when(step == 0)
    def _():
        for c in range(2):
            s_new = jnp.sum(kn_ref[c] * q_ref[c], axis=-1, keepdims=True) * ATTN_SCALE
            m_ref[c] = s_new
            l_ref[c] = jnp.ones(s_new.shape, F32)
            acc_ref[c] = vn_ref[...]

    for c in range(2):
        qc = q_ref[c] * ATTN_SCALE
        s = jnp.concatenate(
            [jnp.sum(kp[p][:, pl.ds(c, DIFF_HEADS, stride=2), :] * qc[None], axis=-1, keepdims=True)
             for p in range(DIFF_PAGES)], axis=0)
        m_old = m_ref[c]
        m_new = jnp.maximum(m_old, jnp.max(s, axis=0))
        alpha = jnp.exp(m_old - m_new)
        pr = jnp.exp(s - m_new[None])
        l_ref[c] = alpha * l_ref[c] + jnp.sum(pr, axis=0)
        pv = jnp.zeros(acc_ref.shape[1:], F32)
        for p in range(DIFF_PAGES):
            pv = pv + jnp.sum(pr[p * PAGE_SIZE:(p + 1) * PAGE_SIZE] * vp[p][...], axis=0)
        acc_ref[c] = alpha * acc_ref[c] + pv
        m_ref[c] = m_new

    @pl.when(step == pl.num_programs(1) - 1)
    def _():
        lam = _lambda_value(lq1_ref, lk1_ref, lq2_ref, lk2_ref, lam_init)
        o = acc_ref[0] / l_ref[0] - lam * (acc_ref[1] / l_ref[1])
        o_ref[...] = _subln(o, g_ref[...], lam_init)


def _diff_sample(q, k_new, v_new, cache_k, cache_v, page_table, lams, g, layer, lam_init):
    db, n_pages = page_table.shape
    steps = n_pages // DIFF_PAGES

    def kspec(kk):
        return pl.BlockSpec((None, None, PAGE_SIZE, 2 * DIFF_HEADS, HEAD_DIM),
                            lambda b, s, pt: (layer, pt[b, s * DIFF_PAGES + kk], 0, 0, 0))

    def vspec(kk):
        return pl.BlockSpec((None, None, PAGE_SIZE, DIFF_HEADS, DIFF_V_DIM),
                            lambda b, s, pt: (layer, pt[b, s * DIFF_PAGES + kk], 0, 0, 0))

    qk = pl.BlockSpec((None, 2, DIFF_HEADS, HEAD_DIM), lambda b, s, pt: (b, 0, 0, 0))
    vo = pl.BlockSpec((None, DIFF_HEADS, DIFF_V_DIM), lambda b, s, pt: (b, 0, 0))
    vec = pl.BlockSpec((1, HEAD_DIM), lambda b, s, pt: (0, 0))
    return pl.pallas_call(
        functools.partial(_diff_sample_kernel, lam_init=lam_init),
        grid_spec=pltpu.PrefetchScalarGridSpec(
            num_scalar_prefetch=1,
            grid=(db, steps),
            in_specs=([kspec(kk) for kk in range(DIFF_PAGES)] + [vspec(kk) for kk in range(DIFF_PAGES)]
                      + [qk, qk, vo, vec, vec, vec, vec,
                         pl.BlockSpec((1, DIFF_V_DIM), lambda b, s, pt: (0, 0))]),
            out_specs=vo,
            scratch_shapes=[pltpu.VMEM((2, DIFF_HEADS, 1), F32), pltpu.VMEM((2, DIFF_HEADS, 1), F32),
                            pltpu.VMEM((2, DIFF_HEADS, DIFF_V_DIM), F32)],
        ),
        out_shape=jax.ShapeDtypeStruct((db, DIFF_HEADS, DIFF_V_DIM), F32),
        compiler_params=_cparams(2),
        name="diff_sample",
    )(page_table, *([cache_k] * DIFF_PAGES), *([cache_v] * DIFF_PAGES), q, k_new, v_new, *lams, g)


def _rope_tables(positions):
    half = HEAD_DIM // 2
    inv_freq = ROPE_THETA ** (-jnp.arange(half, dtype=F32) / half)
    ang = positions.astype(F32)[:, None] * inv_freq[None, :]
    cos, sin = jnp.cos(ang), jnp.sin(ang)
    return jnp.concatenate([cos, cos], axis=-1), jnp.concatenate([-sin, sin], axis=-1)


def kernel(x_prompt, x_sample, cache_moba_k, cache_moba_v, cache_diff_k, cache_diff_v, page_table, w_in, w_out, lambda_q1, lambda_k1, lambda_q2, lambda_k2, subln_g, ln1_g, ln1_b, w_router_group, b_router_group, w_router_expert, b_router_expert, w_exp_gate, w_exp_up, w_exp_down, ln2_g, ln2_b):
    batch, seq, d = x_prompt.shape
    db = x_sample.shape[0]
    depth = w_in.shape[0]
    n_pages = page_table.shape[1]
    past = n_pages * PAGE_SIZE
    m_prompt = batch * seq
    n_valid = m_prompt + db
    m_pad = -(-n_valid // (3 * ROW_TILE)) * (3 * ROW_TILE)
    tm_big = m_pad // 8
    r_max = ((2 * n_valid + N_EXPERTS * (ROW_TILE - 1)) // ROW_TILE + 1) * ROW_TILE

    x = jnp.concatenate([x_prompt.reshape(m_prompt, d), x_sample.reshape(db, d),
                         jnp.zeros((m_pad - n_valid, d), F32)], axis=0)
    xb = x.astype(BF16)
    positions = jnp.concatenate([jnp.tile(jnp.arange(seq, dtype=jnp.int32), batch),
                                 jnp.full((db,), past, jnp.int32),
                                 jnp.zeros((m_pad - n_valid,), jnp.int32)])
    cos, sin = _rope_tables(positions)

    cdk = cache_diff_k.reshape(cache_diff_k.shape[:3] + (2 * DIFF_HEADS, HEAD_DIM))

    outs = [[] for _ in range(8)]
    for l in range(depth):
        lam_init = 0.8 - 0.6 * math.exp(-0.3 * l)
        lams = (lambda_q1[l][None], lambda_k1[l][None], lambda_q2[l][None], lambda_k2[l][None])
        g_sub = subln_g[l][None]

        proj = functools.partial(_project, xb, w_in, l, cos=cos, sin=sin, tm=tm_big, tn=512)
        qm = proj(0 * GROUP_WIDTH, GROUP_WIDTH, rope=True)
        heads = functools.partial(proj, head_rows=m_prompt)
        km, km_h = heads(1 * GROUP_WIDTH, GROUP_WIDTH, rope=True, head_width=HEAD_DIM)
        vm, vm_h = heads(2 * GROUP_WIDTH, GROUP_WIDTH, rope=False, head_width=HEAD_DIM)
        qd = proj(3 * GROUP_WIDTH, GROUP_WIDTH, rope=True)
        kd, kd_h = heads(4 * GROUP_WIDTH, GROUP_WIDTH, rope=True, head_width=HEAD_DIM)
        vd, vd_h = heads(5 * GROUP_WIDTH, GROUP_WIDTH, rope=False, head_width=DIFF_V_DIM)

        om_p = _moba_prompt(qm, km, vm, batch, seq)
        od_p = _diff_prompt(qd, kd, vd, lams, g_sub, batch, seq, lam_init)

        sl = slice(m_prompt, n_valid)
        qm_s = qm[sl].reshape(db, MOBA_HEADS, HEAD_DIM)
        km_s = km[sl].reshape(db, MOBA_HEADS, HEAD_DIM)
        vm_s = vm[sl].reshape(db, MOBA_HEADS, HEAD_DIM)
        k_mean = _kmean(cache_moba_k, page_table, l)
        sel_idx, keep = _sample_topk(k_mean, qm_s)
        blk = jnp.transpose(sel_idx[..., 0], (0, 2, 1))
        logical = blk[..., None] * PAGES_PER_BLOCK + jnp.arange(PAGES_PER_BLOCK, dtype=jnp.int32)
        phys = jnp.take_along_axis(page_table[:, None, :], logical.reshape(db, MOBA_HEADS, -1), axis=2)
        om_s = _moba_sample(qm_s, km_s, vm_s, cache_moba_k, cache_moba_v,
                            phys.reshape(-1).astype(jnp.int32), keep, l)

        qd_s = jnp.transpose(qd[sl].reshape(db, DIFF_HEADS, 2, HEAD_DIM), (0, 2, 1, 3))
        kd_s = jnp.transpose(kd[sl].reshape(db, DIFF_HEADS, 2, HEAD_DIM), (0, 2, 1, 3))
        vd_s = vd[sl].reshape(db, DIFF_HEADS, DIFF_V_DIM)
        od_s = _diff_sample(qd_s, kd_s, vd_s, cdk, cache_diff_v, page_table, lams, g_sub, l, lam_init)

        cat = jnp.concatenate([
            jnp.concatenate([om_p, od_p], axis=1),
            jnp.concatenate([om_s.reshape(db, -1), od_s.reshape(db, -1)], axis=1).astype(BF16),
            jnp.zeros((m_pad - n_valid, d), BF16)], axis=0)
        a = _project(cat, w_out, l, 0, d, cos, sin, rope=False, tm=tm_big, tn=512)

        w_router = jnp.concatenate([w_router_group[l], w_router_expert[l],
                                    jnp.zeros((d, 128 - N_GROUPS - N_EXPERTS), F32)], axis=1)
        b_router = jnp.concatenate([b_router_group[l], b_router_expert[l],
                                    jnp.zeros((128 - N_GROUPS - N_EXPERTS,), F32)])[None]
        h1, h1_packed, route = _ln_route(x, a, ln1_g[l][None], ln1_b[l][None], w_router, b_router, ROW_TILE)

        pos, row_token, row_gate, tile_expert, n_tiles = _dispatch(route, n_valid, r_max)
        x_sorted = _gather_rows(h1_packed, row_token, n_tiles[None], r_max)
        hid = _moe_up(x_sorted, w_exp_gate, w_exp_up, row_gate, tile_expert, n_tiles[None], l, 512)
        y_sorted = _moe_down(hid, w_exp_down, tile_expert, n_tiles[None], l, 2048)
        pos_pad = jnp.concatenate([pos, jnp.zeros((2 * (m_pad - n_valid),), jnp.int32)])
        x, xb = _combine_ln(h1, y_sorted, pos_pad, ln2_g[l][None], ln2_b[l][None])

        outs[0].append(km_h.reshape(batch, seq, MOBA_HEADS, HEAD_DIM))
        outs[1].append(vm_h.reshape(batch, seq, MOBA_HEADS, HEAD_DIM))
        outs[2].append(kd_h.reshape(batch, seq, DIFF_HEADS, 2, HEAD_DIM))
        outs[3].append(vd_h.reshape(batch, seq, DIFF_HEADS, DIFF_V_DIM))
        outs[4].append(km[sl].reshape(db, 1, MOBA_HEADS, HEAD_DIM))
        outs[5].append(vm[sl].reshape(db, 1, MOBA_HEADS, HEAD_DIM))
        outs[6].append(kd[sl].reshape(db, 1, DIFF_HEADS, 2, HEAD_DIM))
        outs[7].append(vd[sl].reshape(db, 1, DIFF_HEADS, DIFF_V_DIM))

    y_prompt = x[:m_prompt].reshape(batch, seq, d)
    y_sample = x[m_prompt:n_valid].reshape(db, 1, d)
    return (y_prompt, y_sample) + tuple(jnp.stack(o) for o in outs)
```
